```python
import jax, jax.numpy as jnp
from jax import lax
import numpy as np

D_MODEL = 1024
BATCH = 1
SEQ = 16384
DEPTH = 4
DEC_BATCH = 32
DEC_SEQ = 16
PAST_LEN = 1024

CHUNK = 64
N_PREV_CHUNKS = 8
BAND = (N_PREV_CHUNKS + 1) * CHUNK
N_HEADS = 8
HEAD_DIM = 64
ATTN_WIDTH = N_HEADS * HEAD_DIM
ATTN_SCALE = HEAD_DIM ** -0.5
REL_CLIP = 128
POOL_WINDOWS = (2, 4, 8, 16)
N_POOL_GROUPS = len(POOL_WINDOWS)
POOL_GROUP = 64
POOL_WIDTH = N_POOL_GROUPS * POOL_GROUP
POOL_HIST = max(POOL_WINDOWS) - 1
CONV_WIDTH = 256
CONV_K = 3
D_FF = 2816
N_BRANCH = 3
IN_WIDTH = 3 * ATTN_WIDTH + POOL_WIDTH + 3 * CONV_WIDTH + N_BRANCH * D_MODEL
SPLITS = (ATTN_WIDTH, 2 * ATTN_WIDTH, 3 * ATTN_WIDTH,
          3 * ATTN_WIDTH + POOL_WIDTH,
          3 * ATTN_WIDTH + POOL_WIDTH + CONV_WIDTH,
          3 * ATTN_WIDTH + POOL_WIDTH + 2 * CONV_WIDTH,
          3 * ATTN_WIDTH + POOL_WIDTH + 3 * CONV_WIDTH)
EPS = 1e-6
NEG_INF = -1e30

kernel_name = 'hybrid_streaming_encoder_step'


def rmsnorm(x, g):
    xf = x.astype(jnp.float32)
    y = xf * lax.rsqrt(jnp.mean(xf * xf, axis=-1, keepdims=True) + EPS)
    return (y * g.astype(jnp.float32)).astype(x.dtype)


def swiglu(h, w_in, w_out):
    g, u = jnp.split(h @ w_in, 2, axis=-1)
    return (jax.nn.silu(g) * u) @ w_out


def rel_bias_table(rel_bias, rel):
    idx = jnp.clip(rel, -REL_CLIP, REL_CLIP) + REL_CLIP
    return rel_bias[:, idx].astype(jnp.float32)


def band_attention_prompt(q, k, v, rel_bias):
    B, S, H, Dh = q.shape
    nc = S // CHUNK
    pad = N_PREV_CHUNKS * CHUNK
    kp = jnp.pad(k, ((0, 0), (pad, 0), (0, 0), (0, 0))).reshape(B, nc + N_PREV_CHUNKS, CHUNK, H, Dh)
    vp = jnp.pad(v, ((0, 0), (pad, 0), (0, 0), (0, 0))).reshape(B, nc + N_PREV_CHUNKS, CHUNK, H, Dh)
    kb = jnp.concatenate([kp[:, i:i + nc] for i in range(N_PREV_CHUNKS + 1)], axis=2)
    vb = jnp.concatenate([vp[:, i:i + nc] for i in range(N_PREV_CHUNKS + 1)], axis=2)
    qc = q.reshape(B, nc, CHUNK, H, Dh)
    q_off = jnp.arange(CHUNK)
    k_off = jnp.arange(BAND) - pad
    bias = rel_bias_table(rel_bias, k_off[None, :] - q_off[:, None])
    valid = (jnp.arange(nc)[:, None] * CHUNK + k_off[None, :]) >= 0
    s = jnp.einsum('bcqhd,bckhd->bchqk', qc, kb).astype(jnp.float32) * ATTN_SCALE + bias[None, None]
    s = jnp.where(valid[None, :, None, None, :], s, NEG_INF)
    p = jax.nn.softmax(s, axis=-1).astype(v.dtype)
    o = jnp.einsum('bchqk,bckhd->bcqhd', p, vb)
    return o.reshape(B, S, H * Dh)


def band_attention_sample(q, k_new, v_new, k_cache, v_cache, rel_bias):
    Bd, T, H, Dh = q.shape
    R = k_cache.shape[1]
    k = jnp.concatenate([k_cache, k_new], axis=1)
    v = jnp.concatenate([v_cache, v_new], axis=1)
    q_off = jnp.arange(T)
    k_off = jnp.arange(R + T) - R
    bias = rel_bias_table(rel_bias, k_off[None, :] - q_off[:, None])
    s = jnp.einsum('bqhd,bkhd->bhqk', q, k).astype(jnp.float32) * ATTN_SCALE + bias[None]
    p = jax.nn.softmax(s, axis=-1).astype(v.dtype)
    o = jnp.einsum('bhqk,bkhd->bqhd', p, v)
    return o.reshape(Bd, T, H * Dh)


def pool_mixer(u, hist, pos0, pool_w, pool_scale):
    B, L, C = u.shape
    ext = jnp.concatenate([hist, u], axis=1).astype(jnp.float32)
    cs = jnp.pad(jnp.cumsum(ext, axis=1), ((0, 0), (1, 0), (0, 0)))
    upper = cs[:, POOL_HIST + 1:]
    pos = pos0 + jnp.arange(L)
    means = []
    for g, w in enumerate(POOL_WINDOWS):
        lo = POOL_HIST + 1 - w
        sl = slice(g * POOL_GROUP, (g + 1) * POOL_GROUP)
        cnt = jnp.minimum(w, pos + 1).astype(jnp.float32)
        means.append((upper[..., sl] - cs[:, lo:lo + L, sl]) / cnt[None, :, None])
    d = (jnp.concatenate(means, axis=-1) - u.astype(jnp.float32)).astype(u.dtype)
    d = d.reshape(B, L, N_POOL_GROUPS, POOL_GROUP)
    o = jnp.einsum('blgc,gcd->blgd', d, pool_w).reshape(B, L, C)
    return o * pool_scale


def causal_dwconv(u, hist, conv_w, conv_b):
    L = u.shape[1]
    ext = jnp.concatenate([hist, u], axis=1)
    y = conv_b
    for i in range(CONV_K):
        y = y + ext[:, i:i + L] * conv_w[i]
    return y


def trunk_layer(x, attn_cache, pool_hist, conv_hist, pos0, w):
    B, L, _ = x.shape
    x = x + 0.5 * swiglu(rmsnorm(x, w['ffn1_norm']), w['ffn1_w_in'], w['ffn1_w_out'])
    h = rmsnorm(x, w['mix_norm'])
    q, k, v, pu, cu, cb, cc, gl = jnp.split(h @ w['mix_w_in'], SPLITS, axis=-1)
    q = rmsnorm(q.reshape(B, L, N_HEADS, HEAD_DIM), w['q_norm'])
    k = rmsnorm(k.reshape(B, L, N_HEADS, HEAD_DIM), w['k_norm'])
    v = v.reshape(B, L, N_HEADS, HEAD_DIM)
    if attn_cache is None:
        o_attn = band_attention_prompt(q, k, v, w['rel_bias'])
        r_keep = min(N_PREV_CHUNKS * CHUNK, L)
        new_k, new_v = k[:, L - r_keep:], v[:, L - r_keep:]
    else:
        o_attn = band_attention_sample(q, k, v, attn_cache[0], attn_cache[1], w['rel_bias'])
        new_k, new_v = k, v
    o_pool = pool_mixer(pu, pool_hist, pos0, w['pool_w'], w['pool_scale'])
    new_pool = jnp.concatenate([pool_hist, pu], axis=1)[:, -POOL_HIST:]
    c_in = cc * cu
    o_conv = cb * causal_dwconv(c_in, conv_hist, w['conv_w'], w['conv_b'])
    new_conv = jnp.concatenate([conv_hist, c_in], axis=1)[:, -(CONV_K - 1):]
    gates = jax.nn.sigmoid(gl + w['gate_b']).reshape(B, L, N_BRANCH, D_MODEL)
    m = (gates[:, :, 0] * (o_attn @ w['w_branch_attn'])
         + gates[:, :, 1] * (o_pool @ w['w_branch_pool'])
         + gates[:, :, 2] * (o_conv @ w['w_branch_conv']))
    x = x + m @ w['w_out']
    x = x + 0.5 * swiglu(rmsnorm(x, w['ffn2_norm']), w['ffn2_w_in'], w['ffn2_w_out'])
    return x, new_k, new_v, new_pool, new_conv


def setup_inputs(seed: int = 0) -> dict:
    key = jax.random.key(seed)
    ks = jax.random.split(key, 32)
    f32 = jnp.float32

    def nrm(k, shape, scale):
        return jax.random.normal(k, shape, f32) * scale

    R = min(N_PREV_CHUNKS * CHUNK, PAST_LEN)
    return {
        'x_prompt': nrm(ks[0], (BATCH, SEQ, D_MODEL), 1.0),
        'x_sample': nrm(ks[1], (DEC_BATCH, DEC_SEQ, D_MODEL), 1.0),
        'cache_attn_k': nrm(ks[2], (DEPTH, DEC_BATCH, R, N_HEADS, HEAD_DIM), 1.0),
        'cache_attn_v': nrm(ks[3], (DEPTH, DEC_BATCH, R, N_HEADS, HEAD_DIM), 1.0),
        'state_pool': nrm(ks[4], (DEPTH, DEC_BATCH, POOL_HIST, POOL_WIDTH), 1.0),
        'state_conv': nrm(ks[5], (DEPTH, DEC_BATCH, CONV_K - 1, CONV_WIDTH), 1.0),
        'ffn1_norm': 1.0 + nrm(ks[6], (DEPTH, D_MODEL), 0.05),
        'ffn1_w_in': nrm(ks[7], (DEPTH, D_MODEL, 2 * D_FF), D_MODEL ** -0.5),
        'ffn1_w_out': nrm(ks[8], (DEPTH, D_FF, D_MODEL), D_FF ** -0.5),
        'mix_norm': 1.0 + nrm(ks[9], (DEPTH, D_MODEL), 0.05),
        'mix_w_in': nrm(ks[10], (DEPTH, D_MODEL, IN_WIDTH), D_MODEL ** -0.5),
        'gate_b': nrm(ks[11], (DEPTH, N_BRANCH * D_MODEL), 0.1),
        'q_norm': 1.0 + nrm(ks[12], (DEPTH, HEAD_DIM), 0.05),
        'k_norm': 1.0 + nrm(ks[13], (DEPTH, HEAD_DIM), 0.05),
        'rel_bias': nrm(ks[14], (DEPTH, N_HEADS, 2 * REL_CLIP + 1), 0.2),
        'pool_w': nrm(ks[15], (DEPTH, N_POOL_GROUPS, POOL_GROUP, POOL_GROUP), POOL_GROUP ** -0.5),
        'pool_scale': 1.0 + nrm(ks[16], (DEPTH, POOL_WIDTH), 0.05),
        'conv_w': nrm(ks[17], (DEPTH, CONV_K, CONV_WIDTH), CONV_K ** -0.5),
        'conv_b': nrm(ks[18], (DEPTH, CONV_WIDTH), 0.02),
        'w_branch_attn': nrm(ks[19], (DEPTH, ATTN_WIDTH, D_MODEL), ATTN_WIDTH ** -0.5),
        'w_branch_pool': nrm(ks[20], (DEPTH, POOL_WIDTH, D_MODEL), POOL_WIDTH ** -0.5),
        'w_branch_conv': nrm(ks[21], (DEPTH, CONV_WIDTH, D_MODEL), CONV_WIDTH ** -0.5),
        'w_out': nrm(ks[22], (DEPTH, D_MODEL, D_MODEL), D_MODEL ** -0.5),
        'ffn2_norm': 1.0 + nrm(ks[23], (DEPTH, D_MODEL), 0.05),
        'ffn2_w_in': nrm(ks[24], (DEPTH, D_MODEL, 2 * D_FF), D_MODEL ** -0.5),
        'ffn2_w_out': nrm(ks[25], (DEPTH, D_FF, D_MODEL), D_FF ** -0.5),
    }


def reference(x_prompt, x_sample, cache_attn_k, cache_attn_v, state_pool, state_conv,
              ffn1_norm, ffn1_w_in, ffn1_w_out, mix_norm, mix_w_in, gate_b, q_norm, k_norm,
              rel_bias, pool_w, pool_scale, conv_w, conv_b, w_branch_attn, w_branch_pool,
              w_branch_conv, w_out, ffn2_norm, ffn2_w_in, ffn2_w_out):
    B = x_prompt.shape[0]
    pool_h0 = jnp.zeros((B, POOL_HIST, POOL_WIDTH), x_prompt.dtype)
    conv_h0 = jnp.zeros((B, CONV_K - 1, CONV_WIDTH), x_prompt.dtype)
    xp, xs = x_prompt, x_sample
    kp_l, vp_l, pp_l, cp_l = [], [], [], []
    ks_l, vs_l, ps_l, cs_l = [], [], [], []
    for l in range(DEPTH):
        w = dict(ffn1_norm=ffn1_norm[l], ffn1_w_in=ffn1_w_in[l], ffn1_w_out=ffn1_w_out[l],
                 mix_norm=mix_norm[l], mix_w_in=mix_w_in[l], gate_b=gate_b[l],
                 q_norm=q_norm[l], k_norm=k_norm[l], rel_bias=rel_bias[l],
                 pool_w=pool_w[l], pool_scale=pool_scale[l], conv_w=conv_w[l], conv_b=conv_b[l],
                 w_branch_attn=w_branch_attn[l], w_branch_pool=w_branch_pool[l],
                 w_branch_conv=w_branch_conv[l], w_out=w_out[l],
                 ffn2_norm=ffn2_norm[l], ffn2_w_in=ffn2_w_in[l], ffn2_w_out=ffn2_w_out[l])
        xp, kp, vp, pp, cp = trunk_layer(xp, None, pool_h0, conv_h0, 0, w)
        xs, k_s, v_s, p_s, c_s = trunk_layer(xs, (cache_attn_k[l], cache_attn_v[l]),
                                            state_pool[l], state_conv[l], PAST_LEN, w)
        kp_l.append(kp); vp_l.append(vp); pp_l.append(pp); cp_l.append(cp)
        ks_l.append(k_s); vs_l.append(v_s); ps_l.append(p_s); cs_l.append(c_s)
    return (xp, xs,
            jnp.stack(kp_l), jnp.stack(vp_l), jnp.stack(pp_l), jnp.stack(cp_l),
            jnp.stack(ks_l), jnp.stack(vs_l), jnp.stack(ps_l), jnp.stack(cs_l))
```

```python
import functools

import jax
import jax.numpy as jnp
import numpy as np
from jax import lax
from jax.experimental import pallas as pl
from jax.experimental.pallas import tpu as pltpu

D_MODEL = 1024
DEPTH = 4
PAST_LEN = 1024
CHUNK = 64
N_PREV_CHUNKS = 8
BAND = (N_PREV_CHUNKS + 1) * CHUNK
N_HEADS = 8
HEAD_DIM = 64
ATTN_WIDTH = N_HEADS * HEAD_DIM
ATTN_SCALE = HEAD_DIM ** -0.5
REL_CLIP = 128
POOL_WINDOWS = (2, 4, 8, 16)
POOL_GROUP = 64
POOL_WIDTH = len(POOL_WINDOWS) * POOL_GROUP
POOL_HIST = max(POOL_WINDOWS) - 1
CONV_WIDTH = 256
CONV_K = 3
D_FF = 2816
EPS = 1e-6
NEG_INF = -1e30

_Q0, _K0, _V0 = 0, ATTN_WIDTH, 2 * ATTN_WIDTH
_P0 = 3 * ATTN_WIDTH
_CU0 = _P0 + POOL_WIDTH
_CB0 = _CU0 + CONV_WIDTH
_CC0 = _CB0 + CONV_WIDTH
_G0 = _CC0 + CONV_WIDTH

TM = N_PREV_CHUNKS * CHUNK
QB = 2 * CHUNK
KB = BAND + CHUNK
_FF_CHUNKS = ((0, 1024), (1024, 2048), (2048, D_FF))
_POOL_PAD = 16
_CONV_PAD = 8

_V7X_VMEM_BYTES = 64 * 2 ** 20
_VMEM_LIMIT_BYTES = _V7X_VMEM_BYTES - 8 * 2 ** 20

F32 = jnp.float32
BF16 = jnp.bfloat16


def _dot(a, b):
    return jnp.dot(a, b, preferred_element_type=F32)


def _dot_nt(a, b):
    return lax.dot_general(a, b, (((1,), (1,)), ((), ())), preferred_element_type=F32)


def _rms(x, g):
    ms = jnp.mean(x * x, axis=-1, keepdims=True)
    return x * lax.rsqrt(ms + EPS) * g


def _head_rms(q, g, head_ones_ref):
    qq = q * q
    hi = qq.astype(BF16)
    lo = (qq - hi.astype(F32)).astype(BF16)
    ss = _dot(hi, head_ones_ref[...]) + _dot(lo, head_ones_ref[...])
    return q * lax.rsqrt(ss * (1.0 / HEAD_DIM) + EPS) * g


def _pool_delta(ext_ref, pu, rows, pos0):
    def back(j):
        return ext_ref[_POOL_PAD - j:_POOL_PAD - j + rows, :]
    s2 = pu + back(1)
    s4 = s2 + back(2) + back(3)
    s8 = s4 + back(4) + back(5) + back(6) + back(7)
    s16 = s8
    for j in range(8, 16):
        s16 = s16 + back(j)
    lane = lax.broadcasted_iota(jnp.int32, (rows, POOL_WIDTH), 1)
    pos = lax.broadcasted_iota(jnp.int32, (rows, POOL_WIDTH), 0) + pos0
    g0, g1, g2 = lane < POOL_GROUP, lane < 2 * POOL_GROUP, lane < 3 * POOL_GROUP
    wsum = jnp.where(g0, s2, jnp.where(g1, s4, jnp.where(g2, s8, s16)))
    wlen = jnp.where(g0, 2, jnp.where(g1, 4, jnp.where(g2, 8, 16)))
    cnt = jnp.minimum(wlen, pos + 1).astype(F32)
    return wsum / cnt - pu


def _conv_taps(ext_ref, rows, cw_ref, cbias_ref):
    y = cbias_ref[...]
    for t in range(CONV_K):
        off = _CONV_PAD - (CONV_K - 1) + t
        y = y + ext_ref[off:off + rows, :] * cw_ref[t:t + 1, :]
    return y


def _merge(x, h, o_attn, o_pool, o_conv, win_ref, gb_ref, wa_ref, wp_ref, wc_ref, wo_ref):
    m = None
    for idx, (o_br, w_ref) in enumerate(((o_attn, wa_ref), (o_pool, wp_ref), (o_conv, wc_ref))):
        lo, hi = _G0 + idx * D_MODEL, _G0 + (idx + 1) * D_MODEL
        gl = _dot(h, win_ref[:, lo:hi]) + gb_ref[:, idx * D_MODEL:(idx + 1) * D_MODEL]
        term = jax.nn.sigmoid(gl) * _dot(o_br.astype(BF16), w_ref[...])
        m = term if m is None else m + term
    return x + _dot(m.astype(BF16), wo_ref[...])


def _ffn_kernel(x_ref, g_ref, win_ref, wout_ref, o_ref):
    x = x_ref[...]
    h = _rms(x, g_ref[...]).astype(BF16)
    acc = None
    for a, b in _FF_CHUNKS:
        gate = _dot(h, win_ref[:, a:b])
        up = _dot(h, win_ref[:, D_FF + a:D_FF + b])
        act = (gate * jax.nn.sigmoid(gate) * up).astype(BF16)
        part = _dot(act, wout_ref[a:b, :])
        acc = part if acc is None else acc + part
    o_ref[...] = x + 0.5 * acc


def _const_spec(shape, layer):
    nd = len(shape)
    return pl.BlockSpec((None,) + tuple(shape), lambda i: (layer,) + (0,) * nd,
                        pipeline_mode=pl.Buffered(1))


def _shared_spec(shape):
    nd = len(shape)
    return pl.BlockSpec(tuple(shape), lambda i: (0,) * nd, pipeline_mode=pl.Buffered(1))


def _ffn(x, norm, w_in, w_out, layer):
    rows = x.shape[0]
    tm = min(TM, rows)
    return pl.pallas_call(
        _ffn_kernel,
        grid=(rows // tm,),
        in_specs=[
            pl.BlockSpec((tm, D_MODEL), lambda i: (i, 0)),
            _const_spec((1, D_MODEL), layer),
            _const_spec((D_MODEL, 2 * D_FF), layer),
            _const_spec((D_FF, D_MODEL), layer),
        ],
        out_specs=pl.BlockSpec((tm, D_MODEL), lambda i: (i, 0)),
        out_shape=jax.ShapeDtypeStruct((rows, D_MODEL), F32),
        compiler_params=pltpu.CompilerParams(
            dimension_semantics=("arbitrary",), vmem_limit_bytes=_VMEM_LIMIT_BYTES),
        name="ffn",
    )(x, norm, w_in, w_out)


def _mix_prompt_kernel(x_ref, g_ref, win_ref, gb_ref, qg_ref, kg_ref, ones_ref, bias_ref,
                       pw_ref, ps_ref, cw_ref, cbias_ref, wa_ref, wp_ref, wc_ref, wo_ref,
                       y_ref, ko_ref, vo_ref, po_ref, co_ref,
                       qbuf, kbuf, vbuf, obuf, pext, cext):
    i = pl.program_id(0)

    @pl.when(i == 0)
    def _():
        kbuf[:, 0:TM, :] = jnp.zeros((N_HEADS, TM, HEAD_DIM), BF16)
        vbuf[:, 0:TM, :] = jnp.zeros((N_HEADS, TM, HEAD_DIM), BF16)
        pext[0:_POOL_PAD, :] = jnp.zeros((_POOL_PAD, POOL_WIDTH), F32)
        cext[0:_CONV_PAD, :] = jnp.zeros((_CONV_PAD, CONV_WIDTH), F32)

    x = x_ref[...]
    h = _rms(x, g_ref[...]).astype(BF16)

    q = _dot(h, win_ref[:, _Q0:_Q0 + ATTN_WIDTH])
    k = _dot(h, win_ref[:, _K0:_K0 + ATTN_WIDTH])
    v = _dot(h, win_ref[:, _V0:_V0 + ATTN_WIDTH])
    qn = _head_rms(q, qg_ref[...], ones_ref) * ATTN_SCALE
    kn = _head_rms(k, kg_ref[...], ones_ref)
    ko_ref[...] = kn
    vo_ref[...] = v
    for hh in range(N_HEADS):
        sl = slice(hh * HEAD_DIM, (hh + 1) * HEAD_DIM)
        qbuf[hh] = qn[:, sl].astype(BF16)
        kbuf[hh, TM:2 * TM, :] = kn[:, sl].astype(BF16)
        vbuf[hh, TM:2 * TM, :] = v[:, sl].astype(BF16)

    first_valid = jnp.where(i > 0, 0, TM)
    for hh in range(N_HEADS):
        def block(j, carry, hh=hh):
            r0 = pl.multiple_of(j * QB, QB)
            qb = qbuf[hh, pl.ds(r0, QB), :]
            kb = kbuf[hh, pl.ds(r0, KB), :]
            vb = vbuf[hh, pl.ds(r0, KB), :]
            s = _dot_nt(qb, kb) + bias_ref[hh]
            col = lax.broadcasted_iota(jnp.int32, (QB, KB), 1)
            s = jnp.where(col >= first_valid - r0, s, NEG_INF)
            p = jnp.exp(s - jnp.max(s, axis=-1, keepdims=True))
            denom = jnp.sum(p, axis=-1, keepdims=True)
            o = _dot(p.astype(BF16), vb) / denom
            obuf[pl.ds(r0, QB), hh * HEAD_DIM:(hh + 1) * HEAD_DIM] = o
            return carry
        lax.fori_loop(0, TM // QB, block, 0)

    pu = _dot(h, win_ref[:, _P0:_P0 + POOL_WIDTH])
    pext[_POOL_PAD:_POOL_PAD + TM, :] = pu
    d = _pool_delta(pext, pu, TM, i * TM)
    o_pool = _dot(d.astype(BF16), pw_ref[...]) * ps_ref[...]
    po_ref[...] = pext[_POOL_PAD + TM - POOL_HIST:_POOL_PAD + TM, :]
    pext[0:_POOL_PAD, :] = pext[TM:TM + _POOL_PAD, :]

    cu = _dot(h, win_ref[:, _CU0:_CU0 + CONV_WIDTH])
    cgate = _dot(h, win_ref[:, _CB0:_CB0 + CONV_WIDTH])
    cc = _dot(h, win_ref[:, _CC0:_CC0 + CONV_WIDTH])
    cext[_CONV_PAD:_CONV_PAD + TM, :] = cc * cu
    o_conv = cgate * _conv_taps(cext, TM, cw_ref, cbias_ref)
    co_ref[...] = cext[_CONV_PAD + TM - (CONV_K - 1):_CONV_PAD + TM, :]
    cext[0:_CONV_PAD, :] = cext[TM:TM + _CONV_PAD, :]

    y_ref[...] = _merge(x, h, obuf[...], o_pool, o_conv, win_ref, gb_ref,
                        wa_ref, wp_ref, wc_ref, wo_ref)

    kbuf[:, 0:TM, :] = kbuf[:, TM:2 * TM, :]
    vbuf[:, 0:TM, :] = vbuf[:, TM:2 * TM, :]


def _mix_prompt(x, layer, p):
    rows = x.shape[0]
    in_w = p["mix_w_in"].shape[-1]
    return pl.pallas_call(
        _mix_prompt_kernel,
        grid=(rows // TM,),
        in_specs=[
            pl.BlockSpec((TM, D_MODEL), lambda i: (i, 0)),
            _const_spec((1, D_MODEL), layer),
            _const_spec((D_MODEL, in_w), layer),
            _const_spec((1, 3 * D_MODEL), layer),
            _const_spec((1, ATTN_WIDTH), layer),
            _const_spec((1, ATTN_WIDTH), layer),
            _shared_spec((ATTN_WIDTH, ATTN_WIDTH)),
            _const_spec((N_HEADS, QB, KB), layer),
            _const_spec((POOL_WIDTH, POOL_WIDTH), layer),
            _const_spec((1, POOL_WIDTH), layer),
            _const_spec((CONV_K, CONV_WIDTH), layer),
            _const_spec((1, CONV_WIDTH), layer),
            _const_spec((ATTN_WIDTH, D_MODEL), layer),
            _const_spec((POOL_WIDTH, D_MODEL), layer),
            _const_spec((CONV_WIDTH, D_MODEL), layer),
            _const_spec((D_MODEL, D_MODEL), layer),
        ],
        out_specs=[
            pl.BlockSpec((TM, D_MODEL), lambda i: (i, 0)),
            pl.BlockSpec((TM, ATTN_WIDTH), lambda i: (0, 0)),
            pl.BlockSpec((TM, ATTN_WIDTH), lambda i: (0, 0)),
            pl.BlockSpec((POOL_HIST, POOL_WIDTH), lambda i: (0, 0)),
            pl.BlockSpec((CONV_K - 1, CONV_WIDTH), lambda i: (0, 0)),
        ],
        out_shape=[
            jax.ShapeDtypeStruct((rows, D_MODEL), F32),
            jax.ShapeDtypeStruct((TM, ATTN_WIDTH), F32),
            jax.ShapeDtypeStruct((TM, ATTN_WIDTH), F32),
            jax.ShapeDtypeStruct((POOL_HIST, POOL_WIDTH), F32),
            jax.ShapeDtypeStruct((CONV_K - 1, CONV_WIDTH), F32),
        ],
        scratch_shapes=[
            pltpu.VMEM((N_HEADS, TM, HEAD_DIM), BF16),
            pltpu.VMEM((N_HEADS, 2 * TM, HEAD_DIM), BF16),
            pltpu.VMEM((N_HEADS, 2 * TM, HEAD_DIM), BF16),
            pltpu.VMEM((TM, ATTN_WIDTH), F32),
            pltpu.VMEM((_POOL_PAD + TM, POOL_WIDTH), F32),
            pltpu.VMEM((_CONV_PAD + TM, CONV_WIDTH), F32),
        ],
        compiler_params=pltpu.CompilerParams(
            dimension_semantics=("arbitrary",), vmem_limit_bytes=_VMEM_LIMIT_BYTES),
        name="mix_prompt",
    )(x, p["mix_norm"], p["mix_w_in"], p["gate_b"], p["q_norm"], p["k_norm"], p["head_ones"],
      p["bias_prompt"], p["pool_w"], p["pool_scale"], p["conv_w"], p["conv_b"],
      p["w_branch_attn"], p["w_branch_pool"], p["w_branch_conv"], p["w_out"])


def _mix_sample_kernel(t_new, x_ref, g_ref, win_ref, gb_ref, qg_ref, kg_ref, ones_ref,
                       b1_ref, b2_ref, hm_ref, kc_ref, vc_ref, sp_ref, sc_ref,
                       pw_ref, ps_ref, cw_ref, cbias_ref, wa_ref, wp_ref, wc_ref, wo_ref,
                       y_ref, ko_ref, vo_ref, po_ref, co_ref,
                       hbuf, qbuf, knbuf, vnbuf, obuf, pubuf, dbuf, cinbuf, ybuf, pext, cext):
    b = pl.program_id(0)
    T = t_new

    @pl.when(b == 0)
    def _():
        h = _rms(x_ref[...], g_ref[...]).astype(BF16)
        hbuf[...] = h
        q = _dot(h, win_ref[:, _Q0:_Q0 + ATTN_WIDTH])
        k = _dot(h, win_ref[:, _K0:_K0 + ATTN_WIDTH])
        v = _dot(h, win_ref[:, _V0:_V0 + ATTN_WIDTH])
        qn = _head_rms(q, qg_ref[...], ones_ref) * ATTN_SCALE
        kn = _head_rms(k, kg_ref[...], ones_ref)
        ko_ref[...] = kn
        vo_ref[...] = v
        qbuf[...] = qn.astype(BF16)
        knbuf[...] = kn.astype(BF16)
        vnbuf[...] = v.astype(BF16)
        pubuf[...] = _dot(h, win_ref[:, _P0:_P0 + POOL_WIDTH])
        cu = _dot(h, win_ref[:, _CU0:_CU0 + CONV_WIDTH])
        cc = _dot(h, win_ref[:, _CC0:_CC0 + CONV_WIDTH])
        cinbuf[...] = cc * cu

    r0 = pl.multiple_of(b * T, T)

    q_b = qbuf[pl.ds(r0, T), :]
    hm = hm_ref[...]
    qexp = jnp.where(hm > 0, jnp.concatenate([q_b] * N_HEADS, axis=0), jnp.zeros((), BF16))
    kc = kc_ref[...].astype(BF16)
    vc = vc_ref[...].astype(BF16)
    s1 = _dot_nt(qexp, kc) + b1_ref[...]
    s2 = _dot_nt(qexp, knbuf[pl.ds(r0, T), :]) + b2_ref[...]
    m = jnp.maximum(jnp.max(s1, axis=-1, keepdims=True), jnp.max(s2, axis=-1, keepdims=True))
    p1 = jnp.exp(s1 - m)
    p2 = jnp.exp(s2 - m)
    denom = jnp.sum(p1, axis=-1, keepdims=True) + jnp.sum(p2, axis=-1, keepdims=True)
    oall = (_dot(p1.astype(BF16), vc) + _dot(p2.astype(BF16), vnbuf[pl.ds(r0, T), :])) / denom
    oall = oall * hm
    o = oall[0:T, :]
    for hh in range(1, N_HEADS):
        o = o + oall[hh * T:(hh + 1) * T, :]
    obuf[pl.ds(r0, T), :] = o

    pu = pubuf[pl.ds(r0, T), :]
    pext[_POOL_PAD - POOL_HIST:_POOL_PAD, :] = sp_ref[b]
    pext[_POOL_PAD:_POOL_PAD + T, :] = pu
    dbuf[pl.ds(r0, T), :] = _pool_delta(pext, pu, T, PAST_LEN)
    po_ref[b] = pext[_POOL_PAD + T - POOL_HIST:_POOL_PAD + T, :]

    cext[_CONV_PAD - (CONV_K - 1):_CONV_PAD, :] = sc_ref[b]
    cext[_CONV_PAD:_CONV_PAD + T, :] = cinbuf[pl.ds(r0, T), :]
    ybuf[pl.ds(r0, T), :] = _conv_taps(cext, T, cw_ref, cbias_ref)
    co_ref[b] = cext[_CONV_PAD + T - (CONV_K - 1):_CONV_PAD + T, :]

    @pl.when(b == pl.num_programs(0) - 1)
    def _():
        h = hbuf[...]
        o_pool = _dot(dbuf[...].astype(BF16), pw_ref[...]) * ps_ref[...]
        o_conv = _dot(h, win_ref[:, _CB0:_CB0 + CONV_WIDTH]) * ybuf[...]
        y_ref[...] = _merge(x_ref[...], h, obuf[...], o_pool, o_conv, win_ref, gb_ref,
                            wa_ref, wp_ref, wc_ref, wo_ref)


def _mix_sample(x, cache_k, cache_v, state_pool, state_conv, layer, p):
    rows = x.shape[0]
    n_streams, r_cache = cache_k.shape[1], cache_k.shape[2]
    t_new = rows // n_streams
    in_w = p["mix_w_in"].shape[-1]
    cache_spec = pl.BlockSpec((None, None, r_cache, ATTN_WIDTH), lambda b: (layer, b, 0, 0))
    full = lambda shape: pl.BlockSpec(tuple(shape), lambda b: (0,) * len(shape))
    return pl.pallas_call(
        functools.partial(_mix_sample_kernel, t_new),
        grid=(n_streams,),
        in_specs=[
            _shared_spec((rows, D_MODEL)),
            _const_spec((1, D_MODEL), layer),
            _const_spec((D_MODEL, in_w), layer),
            _const_spec((1, 3 * D_MODEL), layer),
            _const_spec((1, ATTN_WIDTH), layer),
            _const_spec((1, ATTN_WIDTH), layer),
            _shared_spec((ATTN_WIDTH, ATTN_WIDTH)),
            _const_spec((N_HEADS * t_new, r_cache), layer),
            _const_spec((N_HEADS * t_new, t_new), layer),
            _shared_spec((N_HEADS * t_new, ATTN_WIDTH)),
            cache_spec,
            cache_spec,
            _const_spec((n_streams, POOL_HIST, POOL_WIDTH), layer),
            _const_spec((n_streams, CONV_K - 1, CONV_WIDTH), layer),
            _const_spec((POOL_WIDTH, POOL_WIDTH), layer),
            _const_spec((1, POOL_WIDTH), layer),
            _const_spec((CONV_K, CONV_WIDTH), layer),
            _const_spec((1, CONV_WIDTH), layer),
            _const_spec((ATTN_WIDTH, D_MODEL), layer),
            _const_spec((POOL_WIDTH, D_MODEL), layer),
            _const_spec((CONV_WIDTH, D_MODEL), layer),
            _const_spec((D_MODEL, D_MODEL), layer),
        ],
        out_specs=[
            full((rows, D_MODEL)),
            full((rows, ATTN_WIDTH)),
            full((rows, ATTN_WIDTH)),
            full((n_streams, POOL_HIST, POOL_WIDTH)),
            full((n_streams, CONV_K - 1, CONV_WIDTH)),
        ],
        out_shape=[
            jax.ShapeDtypeStruct((rows, D_MODEL), F32),
            jax.ShapeDtypeStruct((rows, ATTN_WIDTH), F32),
            jax.ShapeDtypeStruct((rows, ATTN_WIDTH), F32),
            jax.ShapeDtypeStruct((n_streams, POOL_HIST, POOL_WIDTH), F32),
            jax.ShapeDtypeStruct((n_streams, CONV_K - 1, CONV_WIDTH), F32),
        ],
        scratch_shapes=[
            pltpu.VMEM((rows, D_MODEL), BF16),
            pltpu.VMEM((rows, ATTN_WIDTH), BF16),
            pltpu.VMEM((rows, ATTN_WIDTH), BF16),
            pltpu.VMEM((rows, ATTN_WIDTH), BF16),
            pltpu.VMEM((rows, ATTN_WIDTH), F32),
            pltpu.VMEM((rows, POOL_WIDTH), F32),
            pltpu.VMEM((rows, POOL_WIDTH), F32),
            pltpu.VMEM((rows, CONV_WIDTH), F32),
            pltpu.VMEM((rows, CONV_WIDTH), F32),
            pltpu.VMEM((_POOL_PAD + t_new, POOL_WIDTH), F32),
            pltpu.VMEM((_CONV_PAD + t_new, CONV_WIDTH), F32),
        ],
        compiler_params=pltpu.CompilerParams(
            dimension_semantics=("arbitrary",), vmem_limit_bytes=_VMEM_LIMIT_BYTES),
        name="mix_sample",
    )(x, p["mix_norm"], p["mix_w_in"], p["gate_b"], p["q_norm"], p["k_norm"], p["head_ones"],
      p["bias_cache"], p["bias_new"], p["head_mask"], cache_k, cache_v, state_pool, state_conv,
      p["pool_w"], p["pool_scale"], p["conv_w"], p["conv_b"],
      p["w_branch_attn"], p["w_branch_pool"], p["w_branch_conv"], p["w_out"])


def _rel_index(rel):
    return np.clip(rel, -REL_CLIP, REL_CLIP) + REL_CLIP


def _prompt_bias(rel_bias):
    r = np.arange(QB)[:, None]
    c = np.arange(KB)[None, :]
    band_col = c - CHUNK * (r // CHUNK)
    in_band = (band_col >= 0) & (band_col < BAND)
    rel = (band_col - N_PREV_CHUNKS * CHUNK) - (r % CHUNK)
    table = rel_bias[:, :, _rel_index(rel)].astype(F32)
    return jnp.where(in_band[None, None], table, NEG_INF)


def _sample_bias(rel_bias, t_new, r_cache):
    t = np.arange(t_new)[:, None]
    rel_cache = (np.arange(r_cache)[None, :] - r_cache) - t
    rel_new = np.arange(t_new)[None, :] - t
    depth = rel_bias.shape[0]
    b1 = rel_bias[:, :, _rel_index(rel_cache)].astype(F32).reshape(depth, N_HEADS * t_new, r_cache)
    b2 = rel_bias[:, :, _rel_index(rel_new)].astype(F32).reshape(depth, N_HEADS * t_new, t_new)
    return b1, b2


def kernel(x_prompt, x_sample, cache_attn_k, cache_attn_v, state_pool, state_conv, ffn1_norm, ffn1_w_in, ffn1_w_out, mix_norm, mix_w_in, gate_b, q_norm, k_norm, rel_bias, pool_w, pool_scale, conv_w, conv_b, w_branch_attn, w_branch_pool, w_branch_conv, w_out, ffn2_norm, ffn2_w_in, ffn2_w_out):
    batch, seq, _ = x_prompt.shape
    n_streams, t_new, _ = x_sample.shape
    depth = mix_w_in.shape[0]
    r_cache = cache_attn_k.shape[2]
    assert batch == 1 and seq % TM == 0 and seq >= TM
    assert r_cache == N_PREV_CHUNKS * CHUNK and PAST_LEN % CHUNK == 0 and t_new <= CHUNK

    row = lambda a: a.reshape(depth, 1, a.shape[-1])
    head = np.arange(ATTN_WIDTH) // HEAD_DIM
    n_groups = len(POOL_WINDOWS)
    pool_bd = jnp.einsum("lgcd,gh->lgchd", pool_w, jnp.eye(n_groups, dtype=pool_w.dtype))
    bias_cache, bias_new = _sample_bias(rel_bias, t_new, r_cache)
    p = dict(
        mix_norm=row(mix_norm), mix_w_in=mix_w_in.astype(BF16), gate_b=row(gate_b),
        q_norm=row(jnp.tile(q_norm, (1, N_HEADS))), k_norm=row(jnp.tile(k_norm, (1, N_HEADS))),
        head_ones=jnp.asarray(head[:, None] == head[None, :], BF16),
        bias_prompt=_prompt_bias(rel_bias), bias_cache=bias_cache, bias_new=bias_new,
        head_mask=jnp.asarray(np.repeat(np.arange(N_HEADS), t_new)[:, None] == head[None, :], F32),
        pool_w=pool_bd.reshape(depth, POOL_WIDTH, POOL_WIDTH).astype(BF16),
        pool_scale=row(pool_scale), conv_w=conv_w, conv_b=row(conv_b),
        w_branch_attn=w_branch_attn.astype(BF16), w_branch_pool=w_branch_pool.astype(BF16),
        w_branch_conv=w_branch_conv.astype(BF16), w_out=w_out.astype(BF16),
    )
    f1 = (row(ffn1_norm), ffn1_w_in.astype(BF16), ffn1_w_out.astype(BF16))
    f2 = (row(ffn2_norm), ffn2_w_in.astype(BF16), ffn2_w_out.astype(BF16))
    cache_k = cache_attn_k.reshape(depth, n_streams, r_cache, ATTN_WIDTH)
    cache_v = cache_attn_v.reshape(depth, n_streams, r_cache, ATTN_WIDTH)

    xp = x_prompt.reshape(seq, D_MODEL)
    xs = x_sample.reshape(n_streams * t_new, D_MODEL)
    outs = [[] for _ in range(8)]
    for l in range(depth):
        xp = _ffn(xp, *f1, l)
        xs = _ffn(xs, *f1, l)
        xp, kp, vp, pp, cp = _mix_prompt(xp, l, p)
        xs, ks, vs, ps, cs = _mix_sample(xs, cache_k, cache_v, state_pool, state_conv, l, p)
        xp = _ffn(xp, *f2, l)
        xs = _ffn(xs, *f2, l)
        for acc, val in zip(outs, (kp, vp, pp, cp, ks, vs, ps, cs)):
            acc.append(val)
    kp, vp, pp, cp, ks, vs, ps, cs = (jnp.stack(o) for o in outs)
    r_keep = min(N_PREV_CHUNKS * CHUNK, seq)
    return (xp.reshape(batch, seq, D_MODEL), xs.reshape(n_streams, t_new, D_MODEL),
            kp.reshape(depth, batch, r_keep, N_HEADS, HEAD_DIM),
            vp.reshape(depth, batch, r_keep, N_HEADS, HEAD_DIM),
            pp.reshape(depth, batch, POOL_HIST, POOL_WIDTH),
            cp.reshape(depth, batch, CONV_K - 1, CONV_WIDTH),
            ks.reshape(depth, n_streams, t_new, N_HEADS, HEAD_DIM),
            vs.reshape(depth, n_streams, t_new, N_HEADS, HEAD_DIM),
            ps, cs)
```

```python
import functools

import jax
import jax.numpy as jnp
import numpy as np
from jax import lax
from jax.experimental import pallas as pl
from jax.experimental.pallas import tpu as pltpu

D_MODEL = 1024
DEPTH = 4
PAST_LEN = 1024
CHUNK = 64
N_PREV_CHUNKS = 8
BAND = (N_PREV_CHUNKS + 1) * CHUNK
N_HEADS = 8
HEAD_DIM = 64
ATTN_WIDTH = N_HEADS * HEAD_DIM
ATTN_SCALE = HEAD_DIM ** -0.5
REL_CLIP = 128
POOL_WINDOWS = (2, 4, 8, 16)
POOL_GROUP = 64
POOL_WIDTH = len(POOL_WINDOWS) * POOL_GROUP
POOL_HIST = max(POOL_WINDOWS) - 1
CONV_WIDTH = 256
CONV_K = 3
D_FF = 2816
EPS = 1e-6
NEG_INF = -1e30

_Q0, _K0, _V0 = 0, ATTN_WIDTH, 2 * ATTN_WIDTH
_P0 = 3 * ATTN_WIDTH
_CU0 = _P0 + POOL_WIDTH
_CB0 = _CU0 + CONV_WIDTH
_CC0 = _CB0 + CONV_WIDTH
_G0 = _CC0 + CONV_WIDTH

TM = N_PREV_CHUNKS * CHUNK
QB = 2 * CHUNK
KB = BAND + CHUNK
_FF_CHUNKS = ((0, 1024), (1024, 2048), (2048, D_FF))
_POOL_PAD = 16
_CONV_PAD = 8

_V7X_VMEM_BYTES = 64 * 2 ** 20
_VMEM_LIMIT_BYTES = _V7X_VMEM_BYTES - 8 * 2 ** 20

F32 = jnp.float32
BF16 = jnp.bfloat16


def _dot(a, b):
    return jnp.dot(a, b, preferred_element_type=F32)


def _dot_nt(a, b):
    return lax.dot_general(a, b, (((1,), (1,)), ((), ())), preferred_element_type=F32)


def _rms(x, g):
    ms = jnp.mean(x * x, axis=-1, keepdims=True)
    return x * lax.rsqrt(ms + EPS) * g


def _head_rms(q, g, head_ones_ref):
    qq = q * q
    hi = qq.astype(BF16)
    lo = (qq - hi.astype(F32)).astype(BF16)
    ss = _dot(hi, head_ones_ref[...]) + _dot(lo, head_ones_ref[...])
    return q * lax.rsqrt(ss * (1.0 / HEAD_DIM) + EPS) * g


def _pool_delta(ext_ref, pu, rows, pos0):
    def back(j):
        return ext_ref[_POOL_PAD - j:_POOL_PAD - j + rows, :]
    s2 = pu + back(1)
    s4 = s2 + back(2) + back(3)
    s8 = s4 + back(4) + back(5) + back(6) + back(7)
    s16 = s8
    for j in range(8, 16):
        s16 = s16 + back(j)
    lane = lax.broadcasted_iota(jnp.int32, (rows, POOL_WIDTH), 1)
    pos = lax.broadcasted_iota(jnp.int32, (rows, POOL_WIDTH), 0) + pos0
    g0, g1, g2 = lane < POOL_GROUP, lane < 2 * POOL_GROUP, lane < 3 * POOL_GROUP
    wsum = jnp.where(g0, s2, jnp.where(g1, s4, jnp.where(g2, s8, s16)))
    wlen = jnp.where(g0, 2, jnp.where(g1, 4, jnp.where(g2, 8, 16)))
    cnt = jnp.minimum(wlen, pos + 1).astype(F32)
    return wsum / cnt - pu


def _conv_taps(ext_ref, rows, cw_ref, cbias_ref):
    y = cbias_ref[...]
    for t in range(CONV_K):
        off = _CONV_PAD - (CONV_K - 1) + t
        y = y + ext_ref[off:off + rows, :] * cw_ref[t:t + 1, :]
    return y


def _merge(x, h, o_attn, o_pool, o_conv, win_ref, gb_ref, wa_ref, wp_ref, wc_ref, wo_ref):
    m = None
    for idx, (o_br, w_ref) in enumerate(((o_attn, wa_ref), (o_pool, wp_ref), (o_conv, wc_ref))):
        lo, hi = _G0 + idx * D_MODEL, _G0 + (idx + 1) * D_MODEL
        gl = _dot(h, win_ref[:, lo:hi]) + gb_ref[:, idx * D_MODEL:(idx + 1) * D_MODEL]
        term = jax.nn.sigmoid(gl) * _dot(o_br.astype(BF16), w_ref[...])
        m = term if m is None else m + term
    return x + _dot(m.astype(BF16), wo_ref[...])


def _ffn_kernel(x_ref, g_ref, win_ref, wout_ref, o_ref):
    x = x_ref[...]
    h = _rms(x, g_ref[...]).astype(BF16)
    acc = None
    for a, b in _FF_CHUNKS:
        gate = _dot(h, win_ref[:, a:b])
        up = _dot(h, win_ref[:, D_FF + a:D_FF + b])
        act = (gate * jax.nn.sigmoid(gate) * up).astype(BF16)
        part = _dot(act, wout_ref[a:b, :])
        acc = part if acc is None else acc + part
    o_ref[...] = x + 0.5 * acc


def _const_spec(shape, layer):
    nd = len(shape)
    return pl.BlockSpec((None,) + tuple(shape), lambda i: (layer,) + (0,) * nd,
                        pipeline_mode=pl.Buffered(1))


def _shared_spec(shape):
    nd = len(shape)
    return pl.BlockSpec(tuple(shape), lambda i: (0,) * nd, pipeline_mode=pl.Buffered(1))


def _ffn(x, norm, w_in, w_out, layer):
    rows = x.shape[0]
    tm = min(TM, rows)
    return pl.pallas_call(
        _ffn_kernel,
        grid=(rows // tm,),
        in_specs=[
            pl.BlockSpec((tm, D_MODEL), lambda i: (i, 0)),
            _const_spec((1, D_MODEL), layer),
            _const_spec((D_MODEL, 2 * D_FF), layer),
            _const_spec((D_FF, D_MODEL), layer),
        ],
        out_specs=pl.BlockSpec((tm, D_MODEL), lambda i: (i, 0)),
        out_shape=jax.ShapeDtypeStruct((rows, D_MODEL), F32),
        compiler_params=pltpu.CompilerParams(
            dimension_semantics=("arbitrary",), vmem_limit_bytes=_VMEM_LIMIT_BYTES),
        name="ffn",
    )(x, norm, w_in, w_out)


def _mix_prompt_kernel(x_ref, g_ref, win_ref, gb_ref, qg_ref, kg_ref, ones_ref, bias_ref,
                       pw_ref, ps_ref, cw_ref, cbias_ref, wa_ref, wp_ref, wc_ref, wo_ref,
                       y_ref, ko_ref, vo_ref, po_ref, co_ref,
                       qbuf, kbuf, vbuf, obuf, pext, cext):
    i = pl.program_id(0)

    @pl.when(i == 0)
    def _():
        kbuf[:, 0:TM, :] = jnp.zeros((N_HEADS, TM, HEAD_DIM), BF16)
        vbuf[:, 0:TM, :] = jnp.zeros((N_HEADS, TM, HEAD_DIM), BF16)
        pext[0:_POOL_PAD, :] = jnp.zeros((_POOL_PAD, POOL_WIDTH), F32)
        cext[0:_CONV_PAD, :] = jnp.zeros((_CONV_PAD, CONV_WIDTH), F32)

    x = x_ref[...]
    h = _rms(x, g_ref[...]).astype(BF16)

    q = _dot(h, win_ref[:, _Q0:_Q0 + ATTN_WIDTH])
    k = _dot(h, win_ref[:, _K0:_K0 + ATTN_WIDTH])
    v = _dot(h, win_ref[:, _V0:_V0 + ATTN_WIDTH])
    qn = _head_rms(q, qg_ref[...], ones_ref) * ATTN_SCALE
    kn = _head_rms(k, kg_ref[...], ones_ref)
    ko_ref[...] = kn
    vo_ref[...] = v
    for hh in range(N_HEADS):
        sl = slice(hh * HEAD_DIM, (hh + 1) * HEAD_DIM)
        qbuf[hh] = qn[:, sl].astype(BF16)
        kbuf[hh, TM:2 * TM, :] = kn[:, sl].astype(BF16)
        vbuf[hh, TM:2 * TM, :] = v[:, sl].astype(BF16)

    first_valid = jnp.where(i > 0, 0, TM)

    def block(j, carry):
        r0 = pl.multiple_of(j * QB, QB)
        col = lax.broadcasted_iota(jnp.int32, (QB, KB), 1)
        exists = col >= first_valid - r0
        for hh in range(N_HEADS):
            qb = qbuf[hh, pl.ds(r0, QB), :]
            kb = kbuf[hh, pl.ds(r0, KB), :]
            vb = vbuf[hh, pl.ds(r0, KB), :]
            s = jnp.where(exists, _dot_nt(qb, kb) + bias_ref[hh], NEG_INF)
            p = jnp.exp(s - jnp.max(s, axis=-1, keepdims=True))
            denom = jnp.sum(p, axis=-1, keepdims=True)
            o = _dot(p.astype(BF16), vb) / denom
            obuf[pl.ds(r0, QB), hh * HEAD_DIM:(hh + 1) * HEAD_DIM] = o
        return carry
    lax.fori_loop(0, TM // QB, block, 0)

    pu = _dot(h, win_ref[:, _P0:_P0 + POOL_WIDTH])
    pext[_POOL_PAD:_POOL_PAD + TM, :] = pu
    d = _pool_delta(pext, pu, TM, i * TM)
    o_pool = _dot(d.astype(BF16), pw_ref[...]) * ps_ref[...]
    po_ref[...] = pext[_POOL_PAD + TM - POOL_HIST:_POOL_PAD + TM, :]
    pext[0:_POOL_PAD, :] = pext[TM:TM + _POOL_PAD, :]

    cu = _dot(h, win_ref[:, _CU0:_CU0 + CONV_WIDTH])
    cgate = _dot(h, win_ref[:, _CB0:_CB0 + CONV_WIDTH])
    cc = _dot(h, win_ref[:, _CC0:_CC0 + CONV_WIDTH])
    cext[_CONV_PAD:_CONV_PAD + TM, :] = cc * cu
    o_conv = cgate * _conv_taps(cext, TM, cw_ref, cbias_ref)
    co_ref[...] = cext[_CONV_PAD + TM - (CONV_K - 1):_CONV_PAD + TM, :]
    cext[0:_CONV_PAD, :] = cext[TM:TM + _CONV_PAD, :]

    y_ref[...] = _merge(x, h, obuf[...], o_pool, o_conv, win_ref, gb_ref,
                        wa_ref, wp_ref, wc_ref, wo_ref)

    kbuf[:, 0:TM, :] = kbuf[:, TM:2 * TM, :]
    vbuf[:, 0:TM, :] = vbuf[:, TM:2 * TM, :]


def _mix_prompt(x, layer, p):
    rows = x.shape[0]
    in_w = p["mix_w_in"].shape[-1]
    return pl.pallas_call(
        _mix_prompt_kernel,
        grid=(rows // TM,),
        in_specs=[
            pl.BlockSpec((TM, D_MODEL), lambda i: (i, 0)),
            _const_spec((1, D_MODEL), layer),
            _const_spec((D_MODEL, in_w), layer),
            _const_spec((1, 3 * D_MODEL), layer),
            _const_spec((1, ATTN_WIDTH), layer),
            _const_spec((1, ATTN_WIDTH), layer),
            _shared_spec((ATTN_WIDTH, ATTN_WIDTH)),
            _const_spec((N_HEADS, QB, KB), layer),
            _const_spec((POOL_WIDTH, POOL_WIDTH), layer),
            _const_spec((1, POOL_WIDTH), layer),
            _const_spec((CONV_K, CONV_WIDTH), layer),
            _const_spec((1, CONV_WIDTH), layer),
            _const_spec((ATTN_WIDTH, D_MODEL), layer),
            _const_spec((POOL_WIDTH, D_MODEL), layer),
            _const_spec((CONV_WIDTH, D_MODEL), layer),
            _const_spec((D_MODEL, D_MODEL), layer),
        ],
        out_specs=[
            pl.BlockSpec((TM, D_MODEL), lambda i: (i, 0)),
            pl.BlockSpec((TM, ATTN_WIDTH), lambda i: (0, 0)),
            pl.BlockSpec((TM, ATTN_WIDTH), lambda i: (0, 0)),
            pl.BlockSpec((POOL_HIST, POOL_WIDTH), lambda i: (0, 0)),
            pl.BlockSpec((CONV_K - 1, CONV_WIDTH), lambda i: (0, 0)),
        ],
        out_shape=[
            jax.ShapeDtypeStruct((rows, D_MODEL), F32),
            jax.ShapeDtypeStruct((TM, ATTN_WIDTH), F32),
            jax.ShapeDtypeStruct((TM, ATTN_WIDTH), F32),
            jax.ShapeDtypeStruct((POOL_HIST, POOL_WIDTH), F32),
            jax.ShapeDtypeStruct((CONV_K - 1, CONV_WIDTH), F32),
        ],
        scratch_shapes=[
            pltpu.VMEM((N_HEADS, TM, HEAD_DIM), BF16),
            pltpu.VMEM((N_HEADS, 2 * TM, HEAD_DIM), BF16),
            pltpu.VMEM((N_HEADS, 2 * TM, HEAD_DIM), BF16),
            pltpu.VMEM((TM, ATTN_WIDTH), F32),
            pltpu.VMEM((_POOL_PAD + TM, POOL_WIDTH), F32),
            pltpu.VMEM((_CONV_PAD + TM, CONV_WIDTH), F32),
        ],
        compiler_params=pltpu.CompilerParams(
            dimension_semantics=("arbitrary",), vmem_limit_bytes=_VMEM_LIMIT_BYTES),
        name="mix_prompt",
    )(x, p["mix_norm"], p["mix_w_in"], p["gate_b"], p["q_norm"], p["k_norm"], p["head_ones"],
      p["bias_prompt"], p["pool_w"], p["pool_scale"], p["conv_w"], p["conv_b"],
      p["w_branch_attn"], p["w_branch_pool"], p["w_branch_conv"], p["w_out"])


def _mix_sample_kernel(t_new, x_ref, g_ref, win_ref, gb_ref, qg_ref, kg_ref, ones_ref,
                       b1_ref, b2_ref, hm_ref, kc_ref, vc_ref, sp_ref, sc_ref,
                       pw_ref, ps_ref, cw_ref, cbias_ref, wa_ref, wp_ref, wc_ref, wo_ref,
                       y_ref, ko_ref, vo_ref, po_ref, co_ref,
                       hbuf, qbuf, knbuf, vnbuf, obuf, pubuf, dbuf, cinbuf, ybuf, pext, cext):
    b = pl.program_id(0)
    T = t_new

    @pl.when(b == 0)
    def _():
        h = _rms(x_ref[...], g_ref[...]).astype(BF16)
        hbuf[...] = h
        q = _dot(h, win_ref[:, _Q0:_Q0 + ATTN_WIDTH])
        k = _dot(h, win_ref[:, _K0:_K0 + ATTN_WIDTH])
        v = _dot(h, win_ref[:, _V0:_V0 + ATTN_WIDTH])
        qn = _head_rms(q, qg_ref[...], ones_ref) * ATTN_SCALE
        kn = _head_rms(k, kg_ref[...], ones_ref)
        ko_ref[...] = kn
        vo_ref[...] = v
        qbuf[...] = qn.astype(BF16)
        knbuf[...] = kn.astype(BF16)
        vnbuf[...] = v.astype(BF16)
        pubuf[...] = _dot(h, win_ref[:, _P0:_P0 + POOL_WIDTH])
        cu = _dot(h, win_ref[:, _CU0:_CU0 + CONV_WIDTH])
        cc = _dot(h, win_ref[:, _CC0:_CC0 + CONV_WIDTH])
        cinbuf[...] = cc * cu

    r0 = pl.multiple_of(b * T, T)

    q_b = qbuf[pl.ds(r0, T), :]
    hm = hm_ref[...]
    qexp = jnp.where(hm > 0, jnp.concatenate([q_b] * N_HEADS, axis=0), jnp.zeros((), BF16))
    kc = kc_ref[...].astype(BF16)
    vc = vc_ref[...].astype(BF16)
    s1 = _dot_nt(qexp, kc) + b1_ref[...]
    s2 = _dot_nt(qexp, knbuf[pl.ds(r0, T), :]) + b2_ref[...]
    m = jnp.maximum(jnp.max(s1, axis=-1, keepdims=True), jnp.max(s2, axis=-1, keepdims=True))
    p1 = jnp.exp(s1 - m)
    p2 = jnp.exp(s2 - m)
    denom = jnp.sum(p1, axis=-1, keepdims=True) + jnp.sum(p2, axis=-1, keepdims=True)
    oall = (_dot(p1.astype(BF16), vc) + _dot(p2.astype(BF16), vnbuf[pl.ds(r0, T), :])) / denom
    oall = oall * hm
    o = oall[0:T, :]
    for hh in range(1, N_HEADS):
        o = o + oall[hh * T:(hh + 1) * T, :]
    obuf[pl.ds(r0, T), :] = o

    pu = pubuf[pl.ds(r0, T), :]
    pext[_POOL_PAD - POOL_HIST:_POOL_PAD, :] = sp_ref[b]
    pext[_POOL_PAD:_POOL_PAD + T, :] = pu
    dbuf[pl.ds(r0, T), :] = _pool_delta(pext, pu, T, PAST_LEN)
    po_ref[b] = pext[_POOL_PAD + T - POOL_HIST:_POOL_PAD + T, :]

    cext[_CONV_PAD - (CONV_K - 1):_CONV_PAD, :] = sc_ref[b]
    cext[_CONV_PAD:_CONV_PAD + T, :] = cinbuf[pl.ds(r0, T), :]
    ybuf[pl.ds(r0, T), :] = _conv_taps(cext, T, cw_ref, cbias_ref)
    co_ref[b] = cext[_CONV_PAD + T - (CONV_K - 1):_CONV_PAD + T, :]

    @pl.when(b == pl.num_programs(0) - 1)
    def _():
        h = hbuf[...]
        o_pool = _dot(dbuf[...].astype(BF16), pw_ref[...]) * ps_ref[...]
        o_conv = _dot(h, win_ref[:, _CB0:_CB0 + CONV_WIDTH]) * ybuf[...]
        y_ref[...] = _merge(x_ref[...], h, obuf[...], o_pool, o_conv, win_ref, gb_ref,
                            wa_ref, wp_ref, wc_ref, wo_ref)


def _mix_sample(x, cache_k, cache_v, state_pool, state_conv, layer, p):
    rows = x.shape[0]
    n_streams, r_cache = cache_k.shape[1], cache_k.shape[2]
    t_new = rows // n_streams
    in_w = p["mix_w_in"].shape[-1]
    cache_spec = pl.BlockSpec((None, None, r_cache, ATTN_WIDTH), lambda b: (layer, b, 0, 0))
    full = lambda shape: pl.BlockSpec(tuple(shape), lambda b: (0,) * len(shape))
    return pl.pallas_call(
        functools.partial(_mix_sample_kernel, t_new),
        grid=(n_streams,),
        in_specs=[
            _shared_spec((rows, D_MODEL)),
            _const_spec((1, D_MODEL), layer),
            _const_spec((D_MODEL, in_w), layer),
            _const_spec((1, 3 * D_MODEL), layer),
            _const_spec((1, ATTN_WIDTH), layer),
            _const_spec((1, ATTN_WIDTH), layer),
            _shared_spec((ATTN_WIDTH, ATTN_WIDTH)),
            _const_spec((N_HEADS * t_new, r_cache), layer),
            _const_spec((N_HEADS * t_new, t_new), layer),
            _shared_spec((N_HEADS * t_new, ATTN_WIDTH)),
            cache_spec,
            cache_spec,
            _const_spec((n_streams, POOL_HIST, POOL_WIDTH), layer),
            _const_spec((n_streams, CONV_K - 1, CONV_WIDTH), layer),
            _const_spec((POOL_WIDTH, POOL_WIDTH), layer),
            _const_spec((1, POOL_WIDTH), layer),
            _const_spec((CONV_K, CONV_WIDTH), layer),
            _const_spec((1, CONV_WIDTH), layer),
            _const_spec((ATTN_WIDTH, D_MODEL), layer),
            _const_spec((POOL_WIDTH, D_MODEL), layer),
            _const_spec((CONV_WIDTH, D_MODEL), layer),
            _const_spec((D_MODEL, D_MODEL), layer),
        ],
        out_specs=[
            full((rows, D_MODEL)),
            full((rows, ATTN_WIDTH)),
            full((rows, ATTN_WIDTH)),
            full((n_streams, POOL_HIST, POOL_WIDTH)),
            full((n_streams, CONV_K - 1, CONV_WIDTH)),
        ],
        out_shape=[
            jax.ShapeDtypeStruct((rows, D_MODEL), F32),
            jax.ShapeDtypeStruct((rows, ATTN_WIDTH), F32),
            jax.ShapeDtypeStruct((rows, ATTN_WIDTH), F32),
            jax.ShapeDtypeStruct((n_streams, POOL_HIST, POOL_WIDTH), F32),
            jax.ShapeDtypeStruct((n_streams, CONV_K - 1, CONV_WIDTH), F32),
        ],
        scratch_shapes=[
            pltpu.VMEM((rows, D_MODEL), BF16),
            pltpu.VMEM((rows, ATTN_WIDTH), BF16),
            pltpu.VMEM((rows, ATTN_WIDTH), BF16),
            pltpu.VMEM((rows, ATTN_WIDTH), BF16),
            pltpu.VMEM((rows, ATTN_WIDTH), F32),
            pltpu.VMEM((rows, POOL_WIDTH), F32),
            pltpu.VMEM((rows, POOL_WIDTH), F32),
            pltpu.VMEM((rows, CONV_WIDTH), F32),
            pltpu.VMEM((rows, CONV_WIDTH), F32),
            pltpu.VMEM((_POOL_PAD + t_new, POOL_WIDTH), F32),
            pltpu.VMEM((_CONV_PAD + t_new, CONV_WIDTH), F32),
        ],
        compiler_params=pltpu.CompilerParams(
            dimension_semantics=("arbitrary",), vmem_limit_bytes=_VMEM_LIMIT_BYTES),
        name="mix_sample",
    )(x, p["mix_norm"], p["mix_w_in"], p["gate_b"], p["q_norm"], p["k_norm"], p["head_ones"],
      p["bias_cache"], p["bias_new"], p["head_mask"], cache_k, cache_v, state_pool, state_conv,
      p["pool_w"], p["pool_scale"], p["conv_w"], p["conv_b"],
      p["w_branch_attn"], p["w_branch_pool"], p["w_branch_conv"], p["w_out"])


def _rel_lookup(rel_bias, lo, hi):
    lead = rel_bias.shape[:-1]
    n_left = max(0, min(hi, -REL_CLIP) - lo)
    n_right = max(0, hi - max(lo, REL_CLIP + 1))
    a, b = max(lo, -REL_CLIP), min(hi, REL_CLIP + 1)
    parts = []
    if n_left:
        parts.append(jnp.broadcast_to(rel_bias[..., :1], lead + (n_left,)))
    if b > a:
        parts.append(rel_bias[..., a + REL_CLIP:b + REL_CLIP])
    if n_right:
        parts.append(jnp.broadcast_to(rel_bias[..., -1:], lead + (n_right,)))
    return jnp.concatenate(parts, axis=-1)


def _toeplitz(g, rows):
    n = g.shape[-1]
    lead = g.shape[:-1]
    t = jnp.broadcast_to(g[..., None, :], lead + (rows, n)).reshape(lead + (rows * n,))
    return t[..., :rows * (n - 1)].reshape(lead + (rows, n - 1))


def _band_bias(rel_bias, n_query, n_past, n_key):
    g = jnp.concatenate([_rel_lookup(rel_bias, -n_past, n_key - n_past),
                         _rel_lookup(rel_bias, -n_past - n_query, -n_past)], axis=-1)
    return _toeplitz(g, n_query)[..., :n_key].astype(F32)


def _prompt_bias(rel_bias):
    band = _band_bias(rel_bias, CHUNK, N_PREV_CHUNKS * CHUNK, BAND)
    masked = jnp.full(band.shape[:-1] + (KB - BAND,), NEG_INF, F32)
    return jnp.concatenate([jnp.concatenate([band, masked], axis=-1),
                            jnp.concatenate([masked, band], axis=-1)], axis=-2)


def _sample_bias(rel_bias, t_new, r_cache):
    depth = rel_bias.shape[0]
    table = _band_bias(rel_bias, t_new, r_cache, r_cache + t_new)
    table = table.reshape(depth, N_HEADS * t_new, r_cache + t_new)
    return table[..., :r_cache], table[..., r_cache:]


def kernel(x_prompt, x_sample, cache_attn_k, cache_attn_v, state_pool, state_conv, ffn1_norm, ffn1_w_in, ffn1_w_out, mix_norm, mix_w_in, gate_b, q_norm, k_norm, rel_bias, pool_w, pool_scale, conv_w, conv_b, w_branch_attn, w_branch_pool, w_branch_conv, w_out, ffn2_norm, ffn2_w_in, ffn2_w_out):
    batch, seq, _ = x_prompt.shape
    n_streams, t_new, _ = x_sample.shape
    depth = mix_w_in.shape[0]
    r_cache = cache_attn_k.shape[2]
    assert batch == 1 and seq % TM == 0 and seq >= TM
    assert r_cache == N_PREV_CHUNKS * CHUNK and PAST_LEN % CHUNK == 0 and t_new <= CHUNK

    row = lambda a: a.reshape(depth, 1, a.shape[-1])
    head = np.arange(ATTN_WIDTH) // HEAD_DIM
    n_groups = len(POOL_WINDOWS)
    pool_bd = jnp.einsum("lgcd,gh->lgchd", pool_w, jnp.eye(n_groups, dtype=pool_w.dtype))
    bias_cache, bias_new = _sample_bias(rel_bias, t_new, r_cache)
    p = dict(
        mix_norm=row(mix_norm), mix_w_in=mix_w_in.astype(BF16), gate_b=row(gate_b),
        q_norm=row(jnp.tile(q_norm, (1, N_HEADS))), k_norm=row(jnp.tile(k_norm, (1, N_HEADS))),
        head_ones=jnp.asarray(head[:, None] == head[None, :], BF16),
        bias_prompt=_prompt_bias(rel_bias), bias_cache=bias_cache, bias_new=bias_new,
        head_mask=jnp.asarray(np.repeat(np.arange(N_HEADS), t_new)[:, None] == head[None, :], F32),
        pool_w=pool_bd.reshape(depth, POOL_WIDTH, POOL_WIDTH).astype(BF16),
        pool_scale=row(pool_scale), conv_w=conv_w, conv_b=row(conv_b),
        w_branch_attn=w_branch_attn.astype(BF16), w_branch_pool=w_branch_pool.astype(BF16),
        w_branch_conv=w_branch_conv.astype(BF16), w_out=w_out.astype(BF16),
    )
    f1 = (row(ffn1_norm), ffn1_w_in.astype(BF16), ffn1_w_out.astype(BF16))
    f2 = (row(ffn2_norm), ffn2_w_in.astype(BF16), ffn2_w_out.astype(BF16))
    cache_k = cache_attn_k.reshape(depth, n_streams, r_cache, ATTN_WIDTH)
    cache_v = cache_attn_v.reshape(depth, n_streams, r_cache, ATTN_WIDTH)

    xp = x_prompt.reshape(seq, D_MODEL)
    xs = x_sample.reshape(n_streams * t_new, D_MODEL)
    outs = [[] for _ in range(8)]
    for l in range(depth):
        xp = _ffn(xp, *f1, l)
        xs = _ffn(xs, *f1, l)
        xp, kp, vp, pp, cp = _mix_prompt(xp, l, p)
        xs, ks, vs, ps, cs = _mix_sample(xs, cache_k, cache_v, state_pool, state_conv, l, p)
        xp = _ffn(xp, *f2, l)
        xs = _ffn(xs, *f2, l)
        for acc, val in zip(outs, (kp, vp, pp, cp, ks, vs, ps, cs)):
            acc.append(val)
    kp, vp, pp, cp, ks, vs, ps, cs = (jnp.stack(o) for o in outs)
    r_keep = min(N_PREV_CHUNKS * CHUNK, seq)
    return (xp.reshape(batch, seq, D_MODEL), xs.reshape(n_streams, t_new, D_MODEL),
            kp.reshape(depth, batch, r_keep, N_HEADS, HEAD_DIM),
            vp.reshape(depth, batch, r_keep, N_HEADS, HEAD_DIM),
            pp.reshape(depth, batch, POOL_HIST, POOL_WIDTH),
            cp.reshape(depth, batch, CONV_K - 1, CONV_WIDTH),
            ks.reshape(depth, n_streams, t_new, N_HEADS, HEAD_DIM),
            vs.reshape(depth, n_streams, t_new, N_HEADS, HEAD_DIM),
            ps, cs)
```

```python
import functools

import jax
import jax.numpy as jnp
import numpy as np
from jax import lax
from jax.experimental import pallas as pl
from jax.experimental.pallas import tpu as pltpu

D_MODEL = 1024
DEPTH = 4
PAST_LEN = 1024
CHUNK = 64
N_PREV_CHUNKS = 8
BAND = (N_PREV_CHUNKS + 1) * CHUNK
N_HEADS = 8
HEAD_DIM = 64
ATTN_WIDTH = N_HEADS * HEAD_DIM
ATTN_SCALE = HEAD_DIM ** -0.5
REL_CLIP = 128
POOL_WINDOWS = (2, 4, 8, 16)
POOL_GROUP = 64
POOL_WIDTH = len(POOL_WINDOWS) * POOL_GROUP
POOL_HIST = max(POOL_WINDOWS) - 1
CONV_WIDTH = 256
CONV_K = 3
D_FF = 2816
EPS = 1e-6
NEG_INF = -1e30

_Q0, _K0, _V0 = 0, ATTN_WIDTH, 2 * ATTN_WIDTH
_P0 = 3 * ATTN_WIDTH
_CU0 = _P0 + POOL_WIDTH
_CB0 = _CU0 + CONV_WIDTH
_CC0 = _CB0 + CONV_WIDTH
_G0 = _CC0 + CONV_WIDTH

TM = N_PREV_CHUNKS * CHUNK
QB = 2 * CHUNK
KB = BAND + CHUNK
QH = 2 * QB
KH = KB + QB
VW = 2 * HEAD_DIM
LOG2E = 1.4426950408889634
_FF_CHUNKS = ((0, 1024), (1024, 2048), (2048, D_FF))
_POOL_PAD = 16
_CONV_PAD = 8

_V7X_VMEM_BYTES = 64 * 2 ** 20
_VMEM_LIMIT_BYTES = _V7X_VMEM_BYTES - 8 * 2 ** 20

F32 = jnp.float32
BF16 = jnp.bfloat16


def _dot(a, b):
    return jnp.dot(a, b, preferred_element_type=F32)


def _dot_nt(a, b):
    return lax.dot_general(a, b, (((1,), (1,)), ((), ())), preferred_element_type=F32)


def _rms(x, g):
    ms = jnp.mean(x * x, axis=-1, keepdims=True)
    return x * lax.rsqrt(ms + EPS) * g


def _head_rms(q, g, head_ones_ref):
    qq = q * q
    hi = qq.astype(BF16)
    lo = (qq - hi.astype(F32)).astype(BF16)
    ss = _dot(hi, head_ones_ref[...]) + _dot(lo, head_ones_ref[...])
    return q * lax.rsqrt(ss * (1.0 / HEAD_DIM) + EPS) * g


def _pool_delta(ext_ref, pu, rows, pos0):
    def back(j):
        return ext_ref[_POOL_PAD - j:_POOL_PAD - j + rows, :]
    s2 = pu + back(1)
    s4 = s2 + back(2) + back(3)
    s8 = s4 + back(4) + back(5) + back(6) + back(7)
    s16 = s8
    for j in range(8, 16):
        s16 = s16 + back(j)
    lane = lax.broadcasted_iota(jnp.int32, (rows, POOL_WIDTH), 1)
    pos = lax.broadcasted_iota(jnp.int32, (rows, POOL_WIDTH), 0) + pos0
    g0, g1, g2 = lane < POOL_GROUP, lane < 2 * POOL_GROUP, lane < 3 * POOL_GROUP
    wsum = jnp.where(g0, s2, jnp.where(g1, s4, jnp.where(g2, s8, s16)))
    wlen = jnp.where(g0, 2, jnp.where(g1, 4, jnp.where(g2, 8, 16)))
    cnt = jnp.minimum(wlen, pos + 1).astype(F32)
    return wsum / cnt - pu


def _conv_taps(ext_ref, rows, cw_ref, cbias_ref):
    y = cbias_ref[...]
    for t in range(CONV_K):
        off = _CONV_PAD - (CONV_K - 1) + t
        y = y + ext_ref[off:off + rows, :] * cw_ref[t:t + 1, :]
    return y


def _merge(x, h, o_attn, o_pool, o_conv, win_ref, gb_ref, wa_ref, wp_ref, wc_ref, wo_ref):
    m = None
    for idx, (o_br, w_ref) in enumerate(((o_attn, wa_ref), (o_pool, wp_ref), (o_conv, wc_ref))):
        lo, hi = _G0 + idx * D_MODEL, _G0 + (idx + 1) * D_MODEL
        gl = _dot(h, win_ref[:, lo:hi]) + gb_ref[:, idx * D_MODEL:(idx + 1) * D_MODEL]
        term = jax.nn.sigmoid(gl) * _dot(o_br.astype(BF16), w_ref[...])
        m = term if m is None else m + term
    return x + _dot(m.astype(BF16), wo_ref[...])


def _ffn_kernel(x_ref, g_ref, win_ref, wout_ref, o_ref):
    x = x_ref[...]
    h = _rms(x, g_ref[...]).astype(BF16)
    acc = None
    for a, b in _FF_CHUNKS:
        gate = _dot(h, win_ref[:, a:b])
        up = _dot(h, win_ref[:, D_FF + a:D_FF + b])
        act = (gate * jax.nn.sigmoid(gate) * up).astype(BF16)
        part = _dot(act, wout_ref[a:b, :])
        acc = part if acc is None else acc + part
    o_ref[...] = x + 0.5 * acc


def _const_spec(shape, layer):
    nd = len(shape)
    return pl.BlockSpec((None,) + tuple(shape), lambda i: (layer,) + (0,) * nd,
                        pipeline_mode=pl.Buffered(1))


def _shared_spec(shape):
    nd = len(shape)
    return pl.BlockSpec(tuple(shape), lambda i: (0,) * nd, pipeline_mode=pl.Buffered(1))


def _ffn(x, norm, w_in, w_out, layer):
    rows = x.shape[0]
    tm = min(TM, rows)
    return pl.pallas_call(
        _ffn_kernel,
        grid=(rows // tm,),
        in_specs=[
            pl.BlockSpec((tm, D_MODEL), lambda i: (i, 0)),
            _const_spec((1, D_MODEL), layer),
            _const_spec((D_MODEL, 2 * D_FF), layer),
            _const_spec((D_FF, D_MODEL), layer),
        ],
        out_specs=pl.BlockSpec((tm, D_MODEL), lambda i: (i, 0)),
        out_shape=jax.ShapeDtypeStruct((rows, D_MODEL), F32),
        compiler_params=pltpu.CompilerParams(
            dimension_semantics=("arbitrary",), vmem_limit_bytes=_VMEM_LIMIT_BYTES),
        name="ffn",
    )(x, norm, w_in, w_out)


def _band_attention(qbuf, kbuf, vbuf, bias_ref, obuf, first_tile):
    def half(hf, carry):
        r0 = pl.multiple_of(hf * QH, QH)
        col = lax.broadcasted_iota(jnp.int32, (QB, KB), 1)
        zero = jnp.zeros((QB, KH - KB), BF16)
        for hh in range(N_HEADS):
            s = _dot_nt(qbuf[hh, pl.ds(r0, QH), :], kbuf[hh, pl.ds(r0, KH), :])
            ps = []
            for a in range(QH // QB):
                sa = s[a * QB:(a + 1) * QB, a * QB:a * QB + KB] + bias_ref[hh]
                if first_tile:
                    sa = jnp.where(col >= TM - r0 - a * QB, sa, NEG_INF)
                ps.append(jnp.exp2(sa - jnp.max(sa, axis=-1, keepdims=True)).astype(BF16))
            p = jnp.concatenate([jnp.concatenate([ps[0], zero], axis=1),
                                 jnp.concatenate([zero, ps[1]], axis=1)], axis=0)
            o = _dot(p, vbuf[hh, pl.ds(r0, KH), :])
            obuf[pl.ds(r0, QH), hh * HEAD_DIM:(hh + 1) * HEAD_DIM] = (
                o[:, :HEAD_DIM] / o[:, HEAD_DIM:HEAD_DIM + 1])
        return carry
    lax.fori_loop(0, TM // QH, half, 0)


def _mix_prompt_kernel(x_ref, g_ref, win_ref, gb_ref, qg_ref, kg_ref, ones_ref, bias_ref,
                       pw_ref, ps_ref, cw_ref, cbias_ref, wa_ref, wp_ref, wc_ref, wo_ref,
                       y_ref, ko_ref, vo_ref, po_ref, co_ref,
                       qbuf, kbuf, vbuf, obuf, pext, cext):
    i = pl.program_id(0)

    @pl.when(i == 0)
    def _():
        kbuf[:, 0:TM, :] = jnp.zeros((N_HEADS, TM, HEAD_DIM), BF16)
        ones_col = lax.broadcasted_iota(jnp.int32, (N_HEADS, 2 * TM, VW), 2) == HEAD_DIM
        vbuf[...] = jnp.where(ones_col, 1.0, 0.0).astype(BF16)
        pext[0:_POOL_PAD, :] = jnp.zeros((_POOL_PAD, POOL_WIDTH), F32)
        cext[0:_CONV_PAD, :] = jnp.zeros((_CONV_PAD, CONV_WIDTH), F32)

    x = x_ref[...]
    h = _rms(x, g_ref[...]).astype(BF16)

    q = _dot(h, win_ref[:, _Q0:_Q0 + ATTN_WIDTH])
    k = _dot(h, win_ref[:, _K0:_K0 + ATTN_WIDTH])
    v = _dot(h, win_ref[:, _V0:_V0 + ATTN_WIDTH])
    qn = _head_rms(q, qg_ref[...], ones_ref) * (ATTN_SCALE * LOG2E)
    kn = _head_rms(k, kg_ref[...], ones_ref)
    ko_ref[...] = kn
    vo_ref[...] = v
    for hh in range(N_HEADS):
        sl = slice(hh * HEAD_DIM, (hh + 1) * HEAD_DIM)
        qbuf[hh] = qn[:, sl].astype(BF16)
        kbuf[hh, TM:2 * TM, :] = kn[:, sl].astype(BF16)
        vbuf[hh, TM:2 * TM, 0:HEAD_DIM] = v[:, sl].astype(BF16)

    @pl.when(i == 0)
    def _():
        _band_attention(qbuf, kbuf, vbuf, bias_ref, obuf, first_tile=True)

    @pl.when(i > 0)
    def _():
        _band_attention(qbuf, kbuf, vbuf, bias_ref, obuf, first_tile=False)

    pu = _dot(h, win_ref[:, _P0:_P0 + POOL_WIDTH])
    pext[_POOL_PAD:_POOL_PAD + TM, :] = pu
    d = _pool_delta(pext, pu, TM, i * TM)
    o_pool = _dot(d.astype(BF16), pw_ref[...]) * ps_ref[...]
    po_ref[...] = pext[_POOL_PAD + TM - POOL_HIST:_POOL_PAD + TM, :]
    pext[0:_POOL_PAD, :] = pext[TM:TM + _POOL_PAD, :]

    cu = _dot(h, win_ref[:, _CU0:_CU0 + CONV_WIDTH])
    cgate = _dot(h, win_ref[:, _CB0:_CB0 + CONV_WIDTH])
    cc = _dot(h, win_ref[:, _CC0:_CC0 + CONV_WIDTH])
    cext[_CONV_PAD:_CONV_PAD + TM, :] = cc * cu
    o_conv = cgate * _conv_taps(cext, TM, cw_ref, cbias_ref)
    co_ref[...] = cext[_CONV_PAD + TM - (CONV_K - 1):_CONV_PAD + TM, :]
    cext[0:_CONV_PAD, :] = cext[TM:TM + _CONV_PAD, :]

    y_ref[...] = _merge(x, h, obuf[...], o_pool, o_conv, win_ref, gb_ref,
                        wa_ref, wp_ref, wc_ref, wo_ref)

    kbuf[:, 0:TM, :] = kbuf[:, TM:2 * TM, :]
    vbuf[:, 0:TM, :] = vbuf[:, TM:2 * TM, :]


def _mix_prompt(x, layer, p):
    rows = x.shape[0]
    in_w = p["mix_w_in"].shape[-1]
    return pl.pallas_call(
        _mix_prompt_kernel,
        grid=(rows // TM,),
        in_specs=[
            pl.BlockSpec((TM, D_MODEL), lambda i: (i, 0)),
            _const_spec((1, D_MODEL), layer),
            _const_spec((D_MODEL, in_w), layer),
            _const_spec((1, 3 * D_MODEL), layer),
            _const_spec((1, ATTN_WIDTH), layer),
            _const_spec((1, ATTN_WIDTH), layer),
            _shared_spec((ATTN_WIDTH, ATTN_WIDTH)),
            _const_spec((N_HEADS, QB, KB), layer),
            _const_spec((POOL_WIDTH, POOL_WIDTH), layer),
            _const_spec((1, POOL_WIDTH), layer),
            _const_spec((CONV_K, CONV_WIDTH), layer),
            _const_spec((1, CONV_WIDTH), layer),
            _const_spec((ATTN_WIDTH, D_MODEL), layer),
            _const_spec((POOL_WIDTH, D_MODEL), layer),
            _const_spec((CONV_WIDTH, D_MODEL), layer),
            _const_spec((D_MODEL, D_MODEL), layer),
        ],
        out_specs=[
            pl.BlockSpec((TM, D_MODEL), lambda i: (i, 0)),
            pl.BlockSpec((TM, ATTN_WIDTH), lambda i: (0, 0)),
            pl.BlockSpec((TM, ATTN_WIDTH), lambda i: (0, 0)),
            pl.BlockSpec((POOL_HIST, POOL_WIDTH), lambda i: (0, 0)),
            pl.BlockSpec((CONV_K - 1, CONV_WIDTH), lambda i: (0, 0)),
        ],
        out_shape=[
            jax.ShapeDtypeStruct((rows, D_MODEL), F32),
            jax.ShapeDtypeStruct((TM, ATTN_WIDTH), F32),
            jax.ShapeDtypeStruct((TM, ATTN_WIDTH), F32),
            jax.ShapeDtypeStruct((POOL_HIST, POOL_WIDTH), F32),
            jax.ShapeDtypeStruct((CONV_K - 1, CONV_WIDTH), F32),
        ],
        scratch_shapes=[
            pltpu.VMEM((N_HEADS, TM, HEAD_DIM), BF16),
            pltpu.VMEM((N_HEADS, 2 * TM, HEAD_DIM), BF16),
            pltpu.VMEM((N_HEADS, 2 * TM, VW), BF16),
            pltpu.VMEM((TM, ATTN_WIDTH), F32),
            pltpu.VMEM((_POOL_PAD + TM, POOL_WIDTH), F32),
            pltpu.VMEM((_CONV_PAD + TM, CONV_WIDTH), F32),
        ],
        compiler_params=pltpu.CompilerParams(
            dimension_semantics=("arbitrary",), vmem_limit_bytes=_VMEM_LIMIT_BYTES),
        name="mix_prompt",
    )(x, p["mix_norm"], p["mix_w_in"], p["gate_b"], p["q_norm"], p["k_norm"], p["head_ones"],
      p["bias_prompt"], p["pool_w"], p["pool_scale"], p["conv_w"], p["conv_b"],
      p["w_branch_attn"], p["w_branch_pool"], p["w_branch_conv"], p["w_out"])


def _mix_sample_kernel(t_new, x_ref, g_ref, win_ref, gb_ref, qg_ref, kg_ref, ones_ref,
                       b1_ref, b2_ref, hm_ref, kc_ref, vc_ref, sp_ref, sc_ref,
                       pw_ref, ps_ref, cw_ref, cbias_ref, wa_ref, wp_ref, wc_ref, wo_ref,
                       y_ref, ko_ref, vo_ref, po_ref, co_ref,
                       hbuf, qbuf, knbuf, vnbuf, obuf, pubuf, dbuf, cinbuf, ybuf, pext, cext):
    b = pl.program_id(0)
    T = t_new

    @pl.when(b == 0)
    def _():
        h = _rms(x_ref[...], g_ref[...]).astype(BF16)
        hbuf[...] = h
        q = _dot(h, win_ref[:, _Q0:_Q0 + ATTN_WIDTH])
        k = _dot(h, win_ref[:, _K0:_K0 + ATTN_WIDTH])
        v = _dot(h, win_ref[:, _V0:_V0 + ATTN_WIDTH])
        qn = _head_rms(q, qg_ref[...], ones_ref) * ATTN_SCALE
        kn = _head_rms(k, kg_ref[...], ones_ref)
        ko_ref[...] = kn
        vo_ref[...] = v
        qbuf[...] = qn.astype(BF16)
        knbuf[...] = kn.astype(BF16)
        vnbuf[...] = v.astype(BF16)
        pubuf[...] = _dot(h, win_ref[:, _P0:_P0 + POOL_WIDTH])
        cu = _dot(h, win_ref[:, _CU0:_CU0 + CONV_WIDTH])
        cc = _dot(h, win_ref[:, _CC0:_CC0 + CONV_WIDTH])
        cinbuf[...] = cc * cu

    r0 = pl.multiple_of(b * T, T)

    q_b = qbuf[pl.ds(r0, T), :]
    hm = hm_ref[...]
    qexp = jnp.where(hm > 0, jnp.concatenate([q_b] * N_HEADS, axis=0), jnp.zeros((), BF16))
    kc = kc_ref[...].astype(BF16)
    vc = vc_ref[...].astype(BF16)
    s1 = _dot_nt(qexp, kc) + b1_ref[...]
    s2 = _dot_nt(qexp, knbuf[pl.ds(r0, T), :]) + b2_ref[...]
    m = jnp.maximum(jnp.max(s1, axis=-1, keepdims=True), jnp.max(s2, axis=-1, keepdims=True))
    p1 = jnp.exp(s1 - m)
    p2 = jnp.exp(s2 - m)
    denom = jnp.sum(p1, axis=-1, keepdims=True) + jnp.sum(p2, axis=-1, keepdims=True)
    oall = (_dot(p1.astype(BF16), vc) + _dot(p2.astype(BF16), vnbuf[pl.ds(r0, T), :])) / denom
    oall = oall * hm
    o = oall[0:T, :]
    for hh in range(1, N_HEADS):
        o = o + oall[hh * T:(hh + 1) * T, :]
    obuf[pl.ds(r0, T), :] = o

    pu = pubuf[pl.ds(r0, T), :]
    pext[_POOL_PAD - POOL_HIST:_POOL_PAD, :] = sp_ref[b]
    pext[_POOL_PAD:_POOL_PAD + T, :] = pu
    dbuf[pl.ds(r0, T), :] = _pool_delta(pext, pu, T, PAST_LEN)
    po_ref[b] = pext[_POOL_PAD + T - POOL_HIST:_POOL_PAD + T, :]

    cext[_CONV_PAD - (CONV_K - 1):_CONV_PAD, :] = sc_ref[b]
    cext[_CONV_PAD:_CONV_PAD + T, :] = cinbuf[pl.ds(r0, T), :]
    ybuf[pl.ds(r0, T), :] = _conv_taps(cext, T, cw_ref, cbias_ref)
    co_ref[b] = cext[_CONV_PAD + T - (CONV_K - 1):_CONV_PAD + T, :]

    @pl.when(b == pl.num_programs(0) - 1)
    def _():
        h = hbuf[...]
        o_pool = _dot(dbuf[...].astype(BF16), pw_ref[...]) * ps_ref[...]
        o_conv = _dot(h, win_ref[:, _CB0:_CB0 + CONV_WIDTH]) * ybuf[...]
        y_ref[...] = _merge(x_ref[...], h, obuf[...], o_pool, o_conv, win_ref, gb_ref,
                            wa_ref, wp_ref, wc_ref, wo_ref)


def _mix_sample(x, cache_k, cache_v, state_pool, state_conv, layer, p):
    rows = x.shape[0]
    n_streams, r_cache = cache_k.shape[1], cache_k.shape[2]
    t_new = rows // n_streams
    in_w = p["mix_w_in"].shape[-1]
    cache_spec = pl.BlockSpec((None, None, r_cache, ATTN_WIDTH), lambda b: (layer, b, 0, 0))
    full = lambda shape: pl.BlockSpec(tuple(shape), lambda b: (0,) * len(shape))
    return pl.pallas_call(
        functools.partial(_mix_sample_kernel, t_new),
        grid=(n_streams,),
        in_specs=[
            _shared_spec((rows, D_MODEL)),
            _const_spec((1, D_MODEL), layer),
            _const_spec((D_MODEL, in_w), layer),
            _const_spec((1, 3 * D_MODEL), layer),
            _const_spec((1, ATTN_WIDTH), layer),
            _const_spec((1, ATTN_WIDTH), layer),
            _shared_spec((ATTN_WIDTH, ATTN_WIDTH)),
            _const_spec((N_HEADS * t_new, r_cache), layer),
            _const_spec((N_HEADS * t_new, t_new), layer),
            _shared_spec((N_HEADS * t_new, ATTN_WIDTH)),
            cache_spec,
            cache_spec,
            _const_spec((n_streams, POOL_HIST, POOL_WIDTH), layer),
            _const_spec((n_streams, CONV_K - 1, CONV_WIDTH), layer),
            _const_spec((POOL_WIDTH, POOL_WIDTH), layer),
            _const_spec((1, POOL_WIDTH), layer),
            _const_spec((CONV_K, CONV_WIDTH), layer),
            _const_spec((1, CONV_WIDTH), layer),
            _const_spec((ATTN_WIDTH, D_MODEL), layer),
            _const_spec((POOL_WIDTH, D_MODEL), layer),
            _const_spec((CONV_WIDTH, D_MODEL), layer),
            _const_spec((D_MODEL, D_MODEL), layer),
        ],
        out_specs=[
            full((rows, D_MODEL)),
            full((rows, ATTN_WIDTH)),
            full((rows, ATTN_WIDTH)),
            full((n_streams, POOL_HIST, POOL_WIDTH)),
            full((n_streams, CONV_K - 1, CONV_WIDTH)),
        ],
        out_shape=[
            jax.ShapeDtypeStruct((rows, D_MODEL), F32),
            jax.ShapeDtypeStruct((rows, ATTN_WIDTH), F32),
            jax.ShapeDtypeStruct((rows, ATTN_WIDTH), F32),
            jax.ShapeDtypeStruct((n_streams, POOL_HIST, POOL_WIDTH), F32),
            jax.ShapeDtypeStruct((n_streams, CONV_K - 1, CONV_WIDTH), F32),
        ],
        scratch_shapes=[
            pltpu.VMEM((rows, D_MODEL), BF16),
            pltpu.VMEM((rows, ATTN_WIDTH), BF16),
            pltpu.VMEM((rows, ATTN_WIDTH), BF16),
            pltpu.VMEM((rows, ATTN_WIDTH), BF16),
            pltpu.VMEM((rows, ATTN_WIDTH), F32),
            pltpu.VMEM((rows, POOL_WIDTH), F32),
            pltpu.VMEM((rows, POOL_WIDTH), F32),
            pltpu.VMEM((rows, CONV_WIDTH), F32),
            pltpu.VMEM((rows, CONV_WIDTH), F32),
            pltpu.VMEM((_POOL_PAD + t_new, POOL_WIDTH), F32),
            pltpu.VMEM((_CONV_PAD + t_new, CONV_WIDTH), F32),
        ],
        compiler_params=pltpu.CompilerParams(
            dimension_semantics=("arbitrary",), vmem_limit_bytes=_VMEM_LIMIT_BYTES),
        name="mix_sample",
    )(x, p["mix_norm"], p["mix_w_in"], p["gate_b"], p["q_norm"], p["k_norm"], p["head_ones"],
      p["bias_cache"], p["bias_new"], p["head_mask"], cache_k, cache_v, state_pool, state_conv,
      p["pool_w"], p["pool_scale"], p["conv_w"], p["conv_b"],
      p["w_branch_attn"], p["w_branch_pool"], p["w_branch_conv"], p["w_out"])


def _rel_lookup(rel_bias, lo, hi):
    lead = rel_bias.shape[:-1]
    n_left = max(0, min(hi, -REL_CLIP) - lo)
    n_right = max(0, hi - max(lo, REL_CLIP + 1))
    a, b = max(lo, -REL_CLIP), min(hi, REL_CLIP + 1)
    parts = []
    if n_left:
        parts.append(jnp.broadcast_to(rel_bias[..., :1], lead + (n_left,)))
    if b > a:
        parts.append(rel_bias[..., a + REL_CLIP:b + REL_CLIP])
    if n_right:
        parts.append(jnp.broadcast_to(rel_bias[..., -1:], lead + (n_right,)))
    return jnp.concatenate(parts, axis=-1)


def _toeplitz(g, rows):
    n = g.shape[-1]
    lead = g.shape[:-1]
    t = jnp.broadcast_to(g[..., None, :], lead + (rows, n)).reshape(lead + (rows * n,))
    return t[..., :rows * (n - 1)].reshape(lead + (rows, n - 1))


def _band_bias(rel_bias, n_query, n_past, n_key):
    g = jnp.concatenate([_rel_lookup(rel_bias, -n_past, n_key - n_past),
                         _rel_lookup(rel_bias, -n_past - n_query, -n_past)], axis=-1)
    return _toeplitz(g, n_query)[..., :n_key].astype(F32)


def _prompt_bias(rel_bias):
    band = _band_bias(rel_bias, CHUNK, N_PREV_CHUNKS * CHUNK, BAND) * LOG2E
    masked = jnp.full(band.shape[:-1] + (KB - BAND,), NEG_INF, F32)
    return jnp.concatenate([jnp.concatenate([band, masked], axis=-1),
                            jnp.concatenate([masked, band], axis=-1)], axis=-2)


def _sample_bias(rel_bias, t_new, r_cache):
    depth = rel_bias.shape[0]
    table = _band_bias(rel_bias, t_new, r_cache, r_cache + t_new)
    table = table.reshape(depth, N_HEADS * t_new, r_cache + t_new)
    return table[..., :r_cache], table[..., r_cache:]


def kernel(x_prompt, x_sample, cache_attn_k, cache_attn_v, state_pool, state_conv, ffn1_norm, ffn1_w_in, ffn1_w_out, mix_norm, mix_w_in, gate_b, q_norm, k_norm, rel_bias, pool_w, pool_scale, conv_w, conv_b, w_branch_attn, w_branch_pool, w_branch_conv, w_out, ffn2_norm, ffn2_w_in, ffn2_w_out):
    batch, seq, _ = x_prompt.shape
    n_streams, t_new, _ = x_sample.shape
    depth = mix_w_in.shape[0]
    r_cache = cache_attn_k.shape[2]
    assert batch == 1 and seq % TM == 0 and seq >= TM
    assert r_cache == N_PREV_CHUNKS * CHUNK and PAST_LEN % CHUNK == 0 and t_new <= CHUNK

    row = lambda a: a.reshape(depth, 1, a.shape[-1])
    head = np.arange(ATTN_WIDTH) // HEAD_DIM
    n_groups = len(POOL_WINDOWS)
    pool_bd = jnp.einsum("lgcd,gh->lgchd", pool_w, jnp.eye(n_groups, dtype=pool_w.dtype))
    bias_cache, bias_new = _sample_bias(rel_bias, t_new, r_cache)
    p = dict(
        mix_norm=row(mix_norm), mix_w_in=mix_w_in.astype(BF16), gate_b=row(gate_b),
        q_norm=row(jnp.tile(q_norm, (1, N_HEADS))), k_norm=row(jnp.tile(k_norm, (1, N_HEADS))),
        head_ones=jnp.asarray(head[:, None] == head[None, :], BF16),
        bias_prompt=_prompt_bias(rel_bias), bias_cache=bias_cache, bias_new=bias_new,
        head_mask=jnp.asarray(np.repeat(np.arange(N_HEADS), t_new)[:, None] == head[None, :], F32),
        pool_w=pool_bd.reshape(depth, POOL_WIDTH, POOL_WIDTH).astype(BF16),
        pool_scale=row(pool_scale), conv_w=conv_w, conv_b=row(conv_b),
        w_branch_attn=w_branch_attn.astype(BF16), w_branch_pool=w_branch_pool.astype(BF16),
        w_branch_conv=w_branch_conv.astype(BF16), w_out=w_out.astype(BF16),
    )
    f1 = (row(ffn1_norm), ffn1_w_in.astype(BF16), ffn1_w_out.astype(BF16))
    f2 = (row(ffn2_norm), ffn2_w_in.astype(BF16), ffn2_w_out.astype(BF16))
    cache_k = cache_attn_k.reshape(depth, n_streams, r_cache, ATTN_WIDTH)
    cache_v = cache_attn_v.reshape(depth, n_streams, r_cache, ATTN_WIDTH)

    xp = x_prompt.reshape(seq, D_MODEL)
    xs = x_sample.reshape(n_streams * t_new, D_MODEL)
    outs = [[] for _ in range(8)]
    for l in range(depth):
        xp = _ffn(xp, *f1, l)
        xs = _ffn(xs, *f1, l)
        xp, kp, vp, pp, cp = _mix_prompt(xp, l, p)
        xs, ks, vs, ps, cs = _mix_sample(xs, cache_k, cache_v, state_pool, state_conv, l, p)
        xp = _ffn(xp, *f2, l)
        xs = _ffn(xs, *f2, l)
        for acc, val in zip(outs, (kp, vp, pp, cp, ks, vs, ps, cs)):
            acc.append(val)
    kp, vp, pp, cp, ks, vs, ps, cs = (jnp.stack(o) for o in outs)
    r_keep = min(N_PREV_CHUNKS * CHUNK, seq)
    return (xp.reshape(batch, seq, D_MODEL), xs.reshape(n_streams, t_new, D_MODEL),
            kp.reshape(depth, batch, r_keep, N_HEADS, HEAD_DIM),
            vp.reshape(depth, batch, r_keep, N_HEADS, HEAD_DIM),
            pp.reshape(depth, batch, POOL_HIST, POOL_WIDTH),
            cp.reshape(depth, batch, CONV_K - 1, CONV_WIDTH),
            ks.reshape(depth, n_streams, t_new, N_HEADS, HEAD_DIM),
            vs.reshape(depth, n_streams, t_new, N_HEADS, HEAD_DIM),
            ps, cs)
```

```python
import functools

import jax
import jax.numpy as jnp
import numpy as np
from jax import lax
from jax.experimental import pallas as pl
from jax.experimental.pallas import tpu as pltpu

D_MODEL = 1024
DEPTH = 4
PAST_LEN = 1024
CHUNK = 64
N_PREV_CHUNKS = 8
BAND = (N_PREV_CHUNKS + 1) * CHUNK
N_HEADS = 8
HEAD_DIM = 64
ATTN_WIDTH = N_HEADS * HEAD_DIM
ATTN_SCALE = HEAD_DIM ** -0.5
REL_CLIP = 128
POOL_WINDOWS = (2, 4, 8, 16)
POOL_GROUP = 64
POOL_WIDTH = len(POOL_WINDOWS) * POOL_GROUP
POOL_HIST = max(POOL_WINDOWS) - 1
CONV_WIDTH = 256
CONV_K = 3
D_FF = 2816
EPS = 1e-6
NEG_INF = -1e30

_Q0, _K0, _V0 = 0, ATTN_WIDTH, 2 * ATTN_WIDTH
_P0 = 3 * ATTN_WIDTH
_CU0 = _P0 + POOL_WIDTH
_CB0 = _CU0 + CONV_WIDTH
_CC0 = _CB0 + CONV_WIDTH
_G0 = _CC0 + CONV_WIDTH

TM = N_PREV_CHUNKS * CHUNK
QB = 2 * CHUNK
KB = BAND + CHUNK
QH = 2 * QB
KH = KB + QB
VW = 2 * HEAD_DIM
LOG2E = 1.4426950408889634
_FF_CHUNKS = ((0, 1024), (1024, 2048), (2048, D_FF))
_POOL_PAD = 16
_CONV_PAD = 8

_V7X_VMEM_BYTES = 64 * 2 ** 20
_VMEM_LIMIT_BYTES = _V7X_VMEM_BYTES - 8 * 2 ** 20

F32 = jnp.float32
BF16 = jnp.bfloat16


def _dot(a, b):
    return jnp.dot(a, b, preferred_element_type=F32)


def _dot_nt(a, b):
    return lax.dot_general(a, b, (((1,), (1,)), ((), ())), preferred_element_type=F32)


def _rms(x, g):
    ms = jnp.mean(x * x, axis=-1, keepdims=True)
    return x * lax.rsqrt(ms + EPS) * g


def _head_rms(q, g, head_ones_ref):
    ss = _dot((q * q).astype(BF16), head_ones_ref[...])
    return q * lax.rsqrt(ss * (1.0 / HEAD_DIM) + EPS) * g


def _pool_delta(ext_ref, pu, rows, pos0):
    def back(j):
        return ext_ref[_POOL_PAD - j:_POOL_PAD - j + rows, :]
    s2 = pu + back(1)
    s4 = s2 + back(2) + back(3)
    s8 = s4 + back(4) + back(5) + back(6) + back(7)
    s16 = s8
    for j in range(8, 16):
        s16 = s16 + back(j)
    lane = lax.broadcasted_iota(jnp.int32, (rows, POOL_WIDTH), 1)
    pos = lax.broadcasted_iota(jnp.int32, (rows, POOL_WIDTH), 0) + pos0
    g0, g1, g2 = lane < POOL_GROUP, lane < 2 * POOL_GROUP, lane < 3 * POOL_GROUP
    wsum = jnp.where(g0, s2, jnp.where(g1, s4, jnp.where(g2, s8, s16)))
    wlen = jnp.where(g0, 2, jnp.where(g1, 4, jnp.where(g2, 8, 16)))
    cnt = jnp.minimum(wlen, pos + 1).astype(F32)
    return wsum / cnt - pu


def _conv_taps(ext_ref, rows, cw_ref, cbias_ref):
    y = cbias_ref[...]
    for t in range(CONV_K):
        off = _CONV_PAD - (CONV_K - 1) + t
        y = y + ext_ref[off:off + rows, :] * cw_ref[t:t + 1, :]
    return y


def _gate(h, idx, win_ref, gb_ref):
    lo, hi = _G0 + idx * D_MODEL, _G0 + (idx + 1) * D_MODEL
    return jax.nn.sigmoid(_dot(h, win_ref[:, lo:hi]) + gb_ref[:, idx * D_MODEL:(idx + 1) * D_MODEL])


def _side_branches(h, o_pool, o_conv, win_ref, gb_ref, wp_ref, wc_ref):
    return (_gate(h, 1, win_ref, gb_ref) * _dot(o_pool.astype(BF16), wp_ref[...])
            + _gate(h, 2, win_ref, gb_ref) * _dot(o_conv.astype(BF16), wc_ref[...]))


def _project_out(x, attn_gate, o_attn, side, wa_ref, wo_ref):
    m = attn_gate * _dot(o_attn.astype(BF16), wa_ref[...]) + side
    return x + _dot(m.astype(BF16), wo_ref[...])


def _ffn_kernel(x_ref, g_ref, win_ref, wout_ref, o_ref):
    x = x_ref[...]
    h = _rms(x, g_ref[...]).astype(BF16)
    acc = None
    for a, b in _FF_CHUNKS:
        gate = _dot(h, win_ref[:, a:b])
        up = _dot(h, win_ref[:, D_FF + a:D_FF + b])
        act = (gate * jax.nn.sigmoid(gate) * up).astype(BF16)
        part = _dot(act, wout_ref[a:b, :])
        acc = part if acc is None else acc + part
    o_ref[...] = x + 0.5 * acc


def _const_spec(shape, layer):
    nd = len(shape)
    return pl.BlockSpec((None,) + tuple(shape), lambda i: (layer,) + (0,) * nd,
                        pipeline_mode=pl.Buffered(1))


def _shared_spec(shape):
    nd = len(shape)
    return pl.BlockSpec(tuple(shape), lambda i: (0,) * nd, pipeline_mode=pl.Buffered(1))


def _ffn(x, norm, w_in, w_out, layer):
    rows = x.shape[0]
    tm = min(TM, rows)
    return pl.pallas_call(
        _ffn_kernel,
        grid=(rows // tm,),
        in_specs=[
            pl.BlockSpec((tm, D_MODEL), lambda i: (i, 0)),
            _const_spec((1, D_MODEL), layer),
            _const_spec((D_MODEL, 2 * D_FF), layer),
            _const_spec((D_FF, D_MODEL), layer),
        ],
        out_specs=pl.BlockSpec((tm, D_MODEL), lambda i: (i, 0)),
        out_shape=jax.ShapeDtypeStruct((rows, D_MODEL), F32),
        compiler_params=pltpu.CompilerParams(
            dimension_semantics=("arbitrary",), vmem_limit_bytes=_VMEM_LIMIT_BYTES),
        name="ffn",
    )(x, norm, w_in, w_out)


def _band_attention(i, qbuf, kbuf, vbuf, bias_ref, obuf, s_bufs, p_bufs):
    n_items = (TM // QH) * N_HEADS
    first_valid = jnp.where(i > 0, 0, TM)
    col = lax.broadcasted_iota(jnp.int32, (QB, KB), 1)

    def item(n):
        return n % N_HEADS, pl.multiple_of((n // N_HEADS) * QH, QH)

    def scores(n, s_ref):
        hh, r0 = item(n)
        s_ref[...] = _dot_nt(qbuf[hh, pl.ds(r0, QH), :], kbuf[hh, pl.ds(r0, KH), :])

    def softmax(n, s_ref, p_ref):
        hh, r0 = item(n)
        for a in range(QH // QB):
            sa = s_ref[a * QB:(a + 1) * QB, a * QB:a * QB + KB] + bias_ref[hh]
            sa = jnp.where(col >= first_valid - r0 - a * QB, sa, NEG_INF)
            p_ref[a * QB:(a + 1) * QB, a * QB:a * QB + KB] = jnp.exp2(
                sa - jnp.max(sa, axis=-1, keepdims=True)).astype(BF16)

    def values(n, p_ref):
        hh, r0 = item(n)
        o = _dot(p_ref[...], vbuf[hh, pl.ds(r0, KH), :])
        obuf[hh, pl.ds(r0, QH), :] = o[:, :HEAD_DIM] / o[:, HEAD_DIM:HEAD_DIM + 1]

    s0, s1 = s_bufs
    p0, p1 = p_bufs
    scores(0, s0)
    softmax(0, s0, p0)
    scores(1, s1)

    def pair(m, carry):
        n = 2 * m + 1
        values(n - 1, p0)
        softmax(n, s1, p1)
        scores(n + 1, s0)
        values(n, p1)
        softmax(n + 1, s0, p0)
        scores(n + 2, s1)
        return carry
    lax.fori_loop(0, n_items // 2 - 1, pair, 0)
    values(n_items - 2, p0)
    softmax(n_items - 1, s1, p1)
    values(n_items - 1, p1)


def _mix_prompt_kernel(x_ref, g_ref, win_ref, gb_ref, qg_ref, kg_ref, ones_ref, bias_ref,
                       pw_ref, ps_ref, cw_ref, cbias_ref, wa_ref, wp_ref, wc_ref, wo_ref,
                       y_ref, ko_ref, vo_ref, po_ref, co_ref,
                       qbuf, kbuf, vbuf, obuf, s0buf, s1buf, p0buf, p1buf, pext, cext):
    i = pl.program_id(0)

    @pl.when(i == 0)
    def _():
        kbuf[:, 0:TM, :] = jnp.zeros((N_HEADS, TM, HEAD_DIM), BF16)
        ones_col = lax.broadcasted_iota(jnp.int32, (N_HEADS, 2 * TM, VW), 2) == HEAD_DIM
        vbuf[...] = jnp.where(ones_col, 1.0, 0.0).astype(BF16)
        pext[0:_POOL_PAD, :] = jnp.zeros((_POOL_PAD, POOL_WIDTH), F32)
        cext[0:_CONV_PAD, :] = jnp.zeros((_CONV_PAD, CONV_WIDTH), F32)
        p0buf[...] = jnp.zeros((QH, KH), BF16)
        p1buf[...] = jnp.zeros((QH, KH), BF16)

    x = x_ref[...]
    h = _rms(x, g_ref[...]).astype(BF16)

    q = _dot(h, win_ref[:, _Q0:_Q0 + ATTN_WIDTH])
    k = _dot(h, win_ref[:, _K0:_K0 + ATTN_WIDTH])
    v = _dot(h, win_ref[:, _V0:_V0 + ATTN_WIDTH])
    qn = _head_rms(q, qg_ref[...], ones_ref) * (ATTN_SCALE * LOG2E)
    kn = _head_rms(k, kg_ref[...], ones_ref)
    ko_ref[...] = kn
    vo_ref[...] = v
    for hh in range(N_HEADS):
        sl = slice(hh * HEAD_DIM, (hh + 1) * HEAD_DIM)
        qbuf[hh] = qn[:, sl].astype(BF16)
        kbuf[hh, TM:2 * TM, :] = kn[:, sl].astype(BF16)
        vbuf[hh, TM:2 * TM, 0:HEAD_DIM] = v[:, sl].astype(BF16)

    pu = _dot(h, win_ref[:, _P0:_P0 + POOL_WIDTH])
    pext[_POOL_PAD:_POOL_PAD + TM, :] = pu
    d = _pool_delta(pext, pu, TM, i * TM)
    o_pool = _dot(d.astype(BF16), pw_ref[...]) * ps_ref[...]
    po_ref[...] = pext[_POOL_PAD + TM - POOL_HIST:_POOL_PAD + TM, :]
    pext[0:_POOL_PAD, :] = pext[TM:TM + _POOL_PAD, :]

    cu = _dot(h, win_ref[:, _CU0:_CU0 + CONV_WIDTH])
    cgate = _dot(h, win_ref[:, _CB0:_CB0 + CONV_WIDTH])
    cc = _dot(h, win_ref[:, _CC0:_CC0 + CONV_WIDTH])
    cext[_CONV_PAD:_CONV_PAD + TM, :] = cc * cu
    o_conv = cgate * _conv_taps(cext, TM, cw_ref, cbias_ref)
    co_ref[...] = cext[_CONV_PAD + TM - (CONV_K - 1):_CONV_PAD + TM, :]
    cext[0:_CONV_PAD, :] = cext[TM:TM + _CONV_PAD, :]

    side = _side_branches(h, o_pool, o_conv, win_ref, gb_ref, wp_ref, wc_ref)
    attn_gate = _gate(h, 0, win_ref, gb_ref)

    _band_attention(i, qbuf, kbuf, vbuf, bias_ref, obuf, (s0buf, s1buf), (p0buf, p1buf))

    o_attn = jnp.concatenate([obuf[hh] for hh in range(N_HEADS)], axis=1)
    y_ref[...] = _project_out(x, attn_gate, o_attn, side, wa_ref, wo_ref)

    kbuf[:, 0:TM, :] = kbuf[:, TM:2 * TM, :]
    vbuf[:, 0:TM, :] = vbuf[:, TM:2 * TM, :]


def _mix_prompt(x, layer, p):
    rows = x.shape[0]
    in_w = p["mix_w_in"].shape[-1]
    return pl.pallas_call(
        _mix_prompt_kernel,
        grid=(rows // TM,),
        in_specs=[
            pl.BlockSpec((TM, D_MODEL), lambda i: (i, 0)),
            _const_spec((1, D_MODEL), layer),
            _const_spec((D_MODEL, in_w), layer),
            _const_spec((1, 3 * D_MODEL), layer),
            _const_spec((1, ATTN_WIDTH), layer),
            _const_spec((1, ATTN_WIDTH), layer),
            _shared_spec((ATTN_WIDTH, ATTN_WIDTH)),
            _const_spec((N_HEADS, QB, KB), layer),
            _const_spec((POOL_WIDTH, POOL_WIDTH), layer),
            _const_spec((1, POOL_WIDTH), layer),
            _const_spec((CONV_K, CONV_WIDTH), layer),
            _const_spec((1, CONV_WIDTH), layer),
            _const_spec((ATTN_WIDTH, D_MODEL), layer),
            _const_spec((POOL_WIDTH, D_MODEL), layer),
            _const_spec((CONV_WIDTH, D_MODEL), layer),
            _const_spec((D_MODEL, D_MODEL), layer),
        ],
        out_specs=[
            pl.BlockSpec((TM, D_MODEL), lambda i: (i, 0)),
            pl.BlockSpec((TM, ATTN_WIDTH), lambda i: (0, 0)),
            pl.BlockSpec((TM, ATTN_WIDTH), lambda i: (0, 0)),
            pl.BlockSpec((POOL_HIST, POOL_WIDTH), lambda i: (0, 0)),
            pl.BlockSpec((CONV_K - 1, CONV_WIDTH), lambda i: (0, 0)),
        ],
        out_shape=[
            jax.ShapeDtypeStruct((rows, D_MODEL), F32),
            jax.ShapeDtypeStruct((TM, ATTN_WIDTH), F32),
            jax.ShapeDtypeStruct((TM, ATTN_WIDTH), F32),
            jax.ShapeDtypeStruct((POOL_HIST, POOL_WIDTH), F32),
            jax.ShapeDtypeStruct((CONV_K - 1, CONV_WIDTH), F32),
        ],
        scratch_shapes=[
            pltpu.VMEM((N_HEADS, TM, HEAD_DIM), BF16),
            pltpu.VMEM((N_HEADS, 2 * TM, HEAD_DIM), BF16),
            pltpu.VMEM((N_HEADS, 2 * TM, VW), BF16),
            pltpu.VMEM((N_HEADS, TM, HEAD_DIM), F32),
            pltpu.VMEM((QH, KH), F32),
            pltpu.VMEM((QH, KH), F32),
            pltpu.VMEM((QH, KH), BF16),
            pltpu.VMEM((QH, KH), BF16),
            pltpu.VMEM((_POOL_PAD + TM, POOL_WIDTH), F32),
            pltpu.VMEM((_CONV_PAD + TM, CONV_WIDTH), F32),
        ],
        compiler_params=pltpu.CompilerParams(
            dimension_semantics=("arbitrary",), vmem_limit_bytes=_VMEM_LIMIT_BYTES),
        name="mix_prompt",
    )(x, p["mix_norm"], p["mix_w_in"], p["gate_b"], p["q_norm"], p["k_norm"], p["head_ones"],
      p["bias_prompt"], p["pool_w"], p["pool_scale"], p["conv_w"], p["conv_b"],
      p["w_branch_attn"], p["w_branch_pool"], p["w_branch_conv"], p["w_out"])


def _mix_sample_kernel(t_new, x_ref, g_ref, win_ref, gb_ref, qg_ref, kg_ref, ones_ref,
                       b1_ref, b2_ref, hm_ref, kc_ref, vc_ref, sp_ref, sc_ref,
                       pw_ref, ps_ref, cw_ref, cbias_ref, wa_ref, wp_ref, wc_ref, wo_ref,
                       y_ref, ko_ref, vo_ref, po_ref, co_ref,
                       hbuf, qbuf, knbuf, vnbuf, obuf, pubuf, dbuf, cinbuf, ybuf, pext, cext):
    b = pl.program_id(0)
    T = t_new

    @pl.when(b == 0)
    def _():
        h = _rms(x_ref[...], g_ref[...]).astype(BF16)
        hbuf[...] = h
        q = _dot(h, win_ref[:, _Q0:_Q0 + ATTN_WIDTH])
        k = _dot(h, win_ref[:, _K0:_K0 + ATTN_WIDTH])
        v = _dot(h, win_ref[:, _V0:_V0 + ATTN_WIDTH])
        qn = _head_rms(q, qg_ref[...], ones_ref) * ATTN_SCALE
        kn = _head_rms(k, kg_ref[...], ones_ref)
        ko_ref[...] = kn
        vo_ref[...] = v
        qbuf[...] = qn.astype(BF16)
        knbuf[...] = kn.astype(BF16)
        vnbuf[...] = v.astype(BF16)
        pubuf[...] = _dot(h, win_ref[:, _P0:_P0 + POOL_WIDTH])
        cu = _dot(h, win_ref[:, _CU0:_CU0 + CONV_WIDTH])
        cc = _dot(h, win_ref[:, _CC0:_CC0 + CONV_WIDTH])
        cinbuf[...] = cc * cu

    r0 = pl.multiple_of(b * T, T)

    q_b = qbuf[pl.ds(r0, T), :]
    hm = hm_ref[...]
    qexp = jnp.where(hm > 0, jnp.concatenate([q_b] * N_HEADS, axis=0), jnp.zeros((), BF16))
    kc = kc_ref[...].astype(BF16)
    vc = vc_ref[...].astype(BF16)
    s1 = _dot_nt(qexp, kc) + b1_ref[...]
    s2 = _dot_nt(qexp, knbuf[pl.ds(r0, T), :]) + b2_ref[...]
    m = jnp.maximum(jnp.max(s1, axis=-1, keepdims=True), jnp.max(s2, axis=-1, keepdims=True))
    p1 = jnp.exp(s1 - m)
    p2 = jnp.exp(s2 - m)
    denom = jnp.sum(p1, axis=-1, keepdims=True) + jnp.sum(p2, axis=-1, keepdims=True)
    oall = (_dot(p1.astype(BF16), vc) + _dot(p2.astype(BF16), vnbuf[pl.ds(r0, T), :])) / denom
    oall = oall * hm
    o = oall[0:T, :]
    for hh in range(1, N_HEADS):
        o = o + oall[hh * T:(hh + 1) * T, :]
    obuf[pl.ds(r0, T), :] = o

    pu = pubuf[pl.ds(r0, T), :]
    pext[_POOL_PAD - POOL_HIST:_POOL_PAD, :] = sp_ref[b]
    pext[_POOL_PAD:_POOL_PAD + T, :] = pu
    dbuf[pl.ds(r0, T), :] = _pool_delta(pext, pu, T, PAST_LEN)
    po_ref[b] = pext[_POOL_PAD + T - POOL_HIST:_POOL_PAD + T, :]

    cext[_CONV_PAD - (CONV_K - 1):_CONV_PAD, :] = sc_ref[b]
    cext[_CONV_PAD:_CONV_PAD + T, :] = cinbuf[pl.ds(r0, T), :]
    ybuf[pl.ds(r0, T), :] = _conv_taps(cext, T, cw_ref, cbias_ref)
    co_ref[b] = cext[_CONV_PAD + T - (CONV_K - 1):_CONV_PAD + T, :]

    @pl.when(b == pl.num_programs(0) - 1)
    def _():
        h = hbuf[...]
        o_pool = _dot(dbuf[...].astype(BF16), pw_ref[...]) * ps_ref[...]
        o_conv = _dot(h, win_ref[:, _CB0:_CB0 + CONV_WIDTH]) * ybuf[...]
        side = _side_branches(h, o_pool, o_conv, win_ref, gb_ref, wp_ref, wc_ref)
        y_ref[...] = _project_out(x_ref[...], _gate(h, 0, win_ref, gb_ref), obuf[...], side,
                                  wa_ref, wo_ref)


def _mix_sample(x, cache_k, cache_v, state_pool, state_conv, layer, p):
    rows = x.shape[0]
    n_streams, r_cache = cache_k.shape[1], cache_k.shape[2]
    t_new = rows // n_streams
    in_w = p["mix_w_in"].shape[-1]
    cache_spec = pl.BlockSpec((None, None, r_cache, ATTN_WIDTH), lambda b: (layer, b, 0, 0))
    full = lambda shape: pl.BlockSpec(tuple(shape), lambda b: (0,) * len(shape))
    return pl.pallas_call(
        functools.partial(_mix_sample_kernel, t_new),
        grid=(n_streams,),
        in_specs=[
            _shared_spec((rows, D_MODEL)),
            _const_spec((1, D_MODEL), layer),
            _const_spec((D_MODEL, in_w), layer),
            _const_spec((1, 3 * D_MODEL), layer),
            _const_spec((1, ATTN_WIDTH), layer),
            _const_spec((1, ATTN_WIDTH), layer),
            _shared_spec((ATTN_WIDTH, ATTN_WIDTH)),
            _const_spec((N_HEADS * t_new, r_cache), layer),
            _const_spec((N_HEADS * t_new, t_new), layer),
            _shared_spec((N_HEADS * t_new, ATTN_WIDTH)),
            cache_spec,
            cache_spec,
            _const_spec((n_streams, POOL_HIST, POOL_WIDTH), layer),
            _const_spec((n_streams, CONV_K - 1, CONV_WIDTH), layer),
            _const_spec((POOL_WIDTH, POOL_WIDTH), layer),
            _const_spec((1, POOL_WIDTH), layer),
            _const_spec((CONV_K, CONV_WIDTH), layer),
            _const_spec((1, CONV_WIDTH), layer),
            _const_spec((ATTN_WIDTH, D_MODEL), layer),
            _const_spec((POOL_WIDTH, D_MODEL), layer),
            _const_spec((CONV_WIDTH, D_MODEL), layer),
            _const_spec((D_MODEL, D_MODEL), layer),
        ],
        out_specs=[
            full((rows, D_MODEL)),
            full((rows, ATTN_WIDTH)),
            full((rows, ATTN_WIDTH)),
            full((n_streams, POOL_HIST, POOL_WIDTH)),
            full((n_streams, CONV_K - 1, CONV_WIDTH)),
        ],
        out_shape=[
            jax.ShapeDtypeStruct((rows, D_MODEL), F32),
            jax.ShapeDtypeStruct((rows, ATTN_WIDTH), F32),
            jax.ShapeDtypeStruct((rows, ATTN_WIDTH), F32),
            jax.ShapeDtypeStruct((n_streams, POOL_HIST, POOL_WIDTH), F32),
            jax.ShapeDtypeStruct((n_streams, CONV_K - 1, CONV_WIDTH), F32),
        ],
        scratch_shapes=[
            pltpu.VMEM((rows, D_MODEL), BF16),
            pltpu.VMEM((rows, ATTN_WIDTH), BF16),
            pltpu.VMEM((rows, ATTN_WIDTH), BF16),
            pltpu.VMEM((rows, ATTN_WIDTH), BF16),
            pltpu.VMEM((rows, ATTN_WIDTH), F32),
            pltpu.VMEM((rows, POOL_WIDTH), F32),
            pltpu.VMEM((rows, POOL_WIDTH), F32),
            pltpu.VMEM((rows, CONV_WIDTH), F32),
            pltpu.VMEM((rows, CONV_WIDTH), F32),
            pltpu.VMEM((_POOL_PAD + t_new, POOL_WIDTH), F32),
            pltpu.VMEM((_CONV_PAD + t_new, CONV_WIDTH), F32),
        ],
        compiler_params=pltpu.CompilerParams(
            dimension_semantics=("arbitrary",), vmem_limit_bytes=_VMEM_LIMIT_BYTES),
        name="mix_sample",
    )(x, p["mix_norm"], p["mix_w_in"], p["gate_b"], p["q_norm"], p["k_norm"], p["head_ones"],
      p["bias_cache"], p["bias_new"], p["head_mask"], cache_k, cache_v, state_pool, state_conv,
      p["pool_w"], p["pool_scale"], p["conv_w"], p["conv_b"],
      p["w_branch_attn"], p["w_branch_pool"], p["w_branch_conv"], p["w_out"])


def _rel_lookup(rel_bias, lo, hi):
    lead = rel_bias.shape[:-1]
    n_left = max(0, min(hi, -REL_CLIP) - lo)
    n_right = max(0, hi - max(lo, REL_CLIP + 1))
    a, b = max(lo, -REL_CLIP), min(hi, REL_CLIP + 1)
    parts = []
    if n_left:
        parts.append(jnp.broadcast_to(rel_bias[..., :1], lead + (n_left,)))
    if b > a:
        parts.append(rel_bias[..., a + REL_CLIP:b + REL_CLIP])
    if n_right:
        parts.append(jnp.broadcast_to(rel_bias[..., -1:], lead + (n_right,)))
    return jnp.concatenate(parts, axis=-1)


def _toeplitz(g, rows):
    n = g.shape[-1]
    lead = g.shape[:-1]
    t = jnp.broadcast_to(g[..., None, :], lead + (rows, n)).reshape(lead + (rows * n,))
    return t[..., :rows * (n - 1)].reshape(lead + (rows, n - 1))


def _band_bias(rel_bias, n_query, n_past, n_key):
    g = jnp.concatenate([_rel_lookup(rel_bias, -n_past, n_key - n_past),
                         _rel_lookup(rel_bias, -n_past - n_query, -n_past)], axis=-1)
    return _toeplitz(g, n_query)[..., :n_key].astype(F32)


def _prompt_bias(rel_bias):
    band = _band_bias(rel_bias, CHUNK, N_PREV_CHUNKS * CHUNK, BAND) * LOG2E
    masked = jnp.full(band.shape[:-1] + (KB - BAND,), NEG_INF, F32)
    return jnp.concatenate([jnp.concatenate([band, masked], axis=-1),
                            jnp.concatenate([masked, band], axis=-1)], axis=-2)


def _sample_bias(rel_bias, t_new, r_cache):
    depth = rel_bias.shape[0]
    table = _band_bias(rel_bias, t_new, r_cache, r_cache + t_new)
    table = table.reshape(depth, N_HEADS * t_new, r_cache + t_new)
    return table[..., :r_cache], table[..., r_cache:]


def kernel(x_prompt, x_sample, cache_attn_k, cache_attn_v, state_pool, state_conv, ffn1_norm, ffn1_w_in, ffn1_w_out, mix_norm, mix_w_in, gate_b, q_norm, k_norm, rel_bias, pool_w, pool_scale, conv_w, conv_b, w_branch_attn, w_branch_pool, w_branch_conv, w_out, ffn2_norm, ffn2_w_in, ffn2_w_out):
    batch, seq, _ = x_prompt.shape
    n_streams, t_new, _ = x_sample.shape
    depth = mix_w_in.shape[0]
    r_cache = cache_attn_k.shape[2]
    assert batch == 1 and seq % TM == 0 and seq >= TM
    assert r_cache == N_PREV_CHUNKS * CHUNK and PAST_LEN % CHUNK == 0 and t_new <= CHUNK

    row = lambda a: a.reshape(depth, 1, a.shape[-1])
    head = np.arange(ATTN_WIDTH) // HEAD_DIM
    n_groups = len(POOL_WINDOWS)
    pool_bd = jnp.einsum("lgcd,gh->lgchd", pool_w, jnp.eye(n_groups, dtype=pool_w.dtype))
    bias_cache, bias_new = _sample_bias(rel_bias, t_new, r_cache)
    p = dict(
        mix_norm=row(mix_norm), mix_w_in=mix_w_in.astype(BF16), gate_b=row(gate_b),
        q_norm=row(jnp.tile(q_norm, (1, N_HEADS))), k_norm=row(jnp.tile(k_norm, (1, N_HEADS))),
        head_ones=jnp.asarray(head[:, None] == head[None, :], BF16),
        bias_prompt=_prompt_bias(rel_bias), bias_cache=bias_cache, bias_new=bias_new,
        head_mask=jnp.asarray(np.repeat(np.arange(N_HEADS), t_new)[:, None] == head[None, :], F32),
        pool_w=pool_bd.reshape(depth, POOL_WIDTH, POOL_WIDTH).astype(BF16),
        pool_scale=row(pool_scale), conv_w=conv_w, conv_b=row(conv_b),
        w_branch_attn=w_branch_attn.astype(BF16), w_branch_pool=w_branch_pool.astype(BF16),
        w_branch_conv=w_branch_conv.astype(BF16), w_out=w_out.astype(BF16),
    )
    f1 = (row(ffn1_norm), ffn1_w_in.astype(BF16), ffn1_w_out.astype(BF16))
    f2 = (row(ffn2_norm), ffn2_w_in.astype(BF16), ffn2_w_out.astype(BF16))
    cache_k = cache_attn_k.reshape(depth, n_streams, r_cache, ATTN_WIDTH)
    cache_v = cache_attn_v.reshape(depth, n_streams, r_cache, ATTN_WIDTH)

    xp = x_prompt.reshape(seq, D_MODEL)
    xs = x_sample.reshape(n_streams * t_new, D_MODEL)
    outs = [[] for _ in range(8)]
    for l in range(depth):
        xp = _ffn(xp, *f1, l)
        xs = _ffn(xs, *f1, l)
        xp, kp, vp, pp, cp = _mix_prompt(xp, l, p)
        xs, ks, vs, ps, cs = _mix_sample(xs, cache_k, cache_v, state_pool, state_conv, l, p)
        xp = _ffn(xp, *f2, l)
        xs = _ffn(xs, *f2, l)
        for acc, val in zip(outs, (kp, vp, pp, cp, ks, vs, ps, cs)):
            acc.append(val)
    kp, vp, pp, cp, ks, vs, ps, cs = (jnp.stack(o) for o in outs)
    r_keep = min(N_PREV_CHUNKS * CHUNK, seq)
    return (xp.reshape(batch, seq, D_MODEL), xs.reshape(n_streams, t_new, D_MODEL),
            kp.reshape(depth, batch, r_keep, N_HEADS, HEAD_DIM),
            vp.reshape(depth, batch, r_keep, N_HEADS, HEAD_DIM),
            pp.reshape(depth, batch, POOL_HIST, POOL_WIDTH),
            cp.reshape(depth, batch, CONV_K - 1, CONV_WIDTH),
            ks.reshape(depth, n_streams, t_new, N_HEADS, HEAD_DIM),
            vs.reshape(depth, n_streams, t_new, N_HEADS, HEAD_DIM),
            ps, cs)
```

```python
import functools

import jax
import jax.numpy as jnp
import numpy as np
from jax import lax
from jax.experimental import pallas as pl
from jax.experimental.pallas import tpu as pltpu

D_MODEL = 1024
DEPTH = 4
PAST_LEN = 1024
CHUNK = 64
N_PREV_CHUNKS = 8
BAND = (N_PREV_CHUNKS + 1) * CHUNK
N_HEADS = 8
HEAD_DIM = 64
ATTN_WIDTH = N_HEADS * HEAD_DIM
ATTN_SCALE = HEAD_DIM ** -0.5
REL_CLIP = 128
POOL_WINDOWS = (2, 4, 8, 16)
POOL_GROUP = 64
POOL_WIDTH = len(POOL_WINDOWS) * POOL_GROUP
POOL_HIST = max(POOL_WINDOWS) - 1
CONV_WIDTH = 256
CONV_K = 3
D_FF = 2816
EPS = 1e-6
NEG_INF = -1e30

_Q0, _K0, _V0 = 0, ATTN_WIDTH, 2 * ATTN_WIDTH
_P0 = 3 * ATTN_WIDTH
_CU0 = _P0 + POOL_WIDTH
_CB0 = _CU0 + CONV_WIDTH
_CC0 = _CB0 + CONV_WIDTH
_G0 = _CC0 + CONV_WIDTH

TM = N_PREV_CHUNKS * CHUNK
QB = 2 * CHUNK
KB = BAND + CHUNK
QH = 2 * QB
KH = KB + QB
VW = 2 * HEAD_DIM
LOG2E = 1.4426950408889634
_FF_CHUNKS = ((0, 1024), (1024, 2048), (2048, D_FF))
_FFN_TM = 2 * TM
_POOL_PAD = 16
_CONV_PAD = 8

_V7X_VMEM_BYTES = 64 * 2 ** 20
_VMEM_LIMIT_BYTES = _V7X_VMEM_BYTES - 8 * 2 ** 20

F32 = jnp.float32
BF16 = jnp.bfloat16


def _dot(a, b):
    return jnp.dot(a, b, preferred_element_type=F32)


def _dot_nt(a, b):
    return lax.dot_general(a, b, (((1,), (1,)), ((), ())), preferred_element_type=F32)


def _rms(x, g):
    ms = jnp.mean(x * x, axis=-1, keepdims=True)
    return x * lax.rsqrt(ms + EPS) * g


def _head_rms(q, g, head_ones_ref):
    ss = _dot((q * q).astype(BF16), head_ones_ref[...])
    return q * lax.rsqrt(ss * (1.0 / HEAD_DIM) + EPS) * g


def _pool_delta(ext_ref, pu, rows, pos0):
    def back(j):
        return ext_ref[_POOL_PAD - j:_POOL_PAD - j + rows, :]
    s2 = pu + back(1)
    s4 = s2 + back(2) + back(3)
    s8 = s4 + back(4) + back(5) + back(6) + back(7)
    s16 = s8
    for j in range(8, 16):
        s16 = s16 + back(j)
    lane = lax.broadcasted_iota(jnp.int32, (rows, POOL_WIDTH), 1)
    pos = lax.broadcasted_iota(jnp.int32, (rows, POOL_WIDTH), 0) + pos0
    g0, g1, g2 = lane < POOL_GROUP, lane < 2 * POOL_GROUP, lane < 3 * POOL_GROUP
    wsum = jnp.where(g0, s2, jnp.where(g1, s4, jnp.where(g2, s8, s16)))
    wlen = jnp.where(g0, 2, jnp.where(g1, 4, jnp.where(g2, 8, 16)))
    cnt = jnp.minimum(wlen, pos + 1).astype(F32)
    return wsum / cnt - pu


def _conv_taps(ext_ref, rows, cw_ref, cbias_ref):
    y = cbias_ref[...]
    for t in range(CONV_K):
        off = _CONV_PAD - (CONV_K - 1) + t
        y = y + ext_ref[off:off + rows, :] * cw_ref[t:t + 1, :]
    return y


def _gate(h, idx, win_ref, gb_ref):
    lo, hi = _G0 + idx * D_MODEL, _G0 + (idx + 1) * D_MODEL
    return jax.nn.sigmoid(_dot(h, win_ref[:, lo:hi]) + gb_ref[:, idx * D_MODEL:(idx + 1) * D_MODEL])


def _side_branches(h, o_pool, o_conv, win_ref, gb_ref, wp_ref, wc_ref):
    return (_gate(h, 1, win_ref, gb_ref) * _dot(o_pool.astype(BF16), wp_ref[...])
            + _gate(h, 2, win_ref, gb_ref) * _dot(o_conv.astype(BF16), wc_ref[...]))


def _project_out(x, attn_gate, o_attn, side, wa_ref, wo_ref):
    m = attn_gate * _dot(o_attn.astype(BF16), wa_ref[...]) + side
    return x + _dot(m.astype(BF16), wo_ref[...])


def _ffn_kernel(x_ref, g_ref, win_ref, wout_ref, o_ref):
    x = x_ref[...]
    h = _rms(x, g_ref[...]).astype(BF16)
    acc = None
    for a, b in _FF_CHUNKS:
        gate = _dot(h, win_ref[:, a:b])
        up = _dot(h, win_ref[:, D_FF + a:D_FF + b])
        act = (gate * jax.nn.sigmoid(gate) * up).astype(BF16)
        part = _dot(act, wout_ref[a:b, :])
        acc = part if acc is None else acc + part
    o_ref[...] = x + 0.5 * acc


def _const_spec(shape, layer):
    nd = len(shape)
    return pl.BlockSpec((None,) + tuple(shape), lambda i: (layer,) + (0,) * nd,
                        pipeline_mode=pl.Buffered(1))


def _shared_spec(shape):
    nd = len(shape)
    return pl.BlockSpec(tuple(shape), lambda i: (0,) * nd, pipeline_mode=pl.Buffered(1))


def _ffn(x, norm, w_in, w_out, layer):
    rows = x.shape[0]
    tm = min(_FFN_TM, rows)
    return pl.pallas_call(
        _ffn_kernel,
        grid=(rows // tm,),
        in_specs=[
            pl.BlockSpec((tm, D_MODEL), lambda i: (i, 0)),
            _const_spec((1, D_MODEL), layer),
            _const_spec((D_MODEL, 2 * D_FF), layer),
            _const_spec((D_FF, D_MODEL), layer),
        ],
        out_specs=pl.BlockSpec((tm, D_MODEL), lambda i: (i, 0)),
        out_shape=jax.ShapeDtypeStruct((rows, D_MODEL), F32),
        compiler_params=pltpu.CompilerParams(
            dimension_semantics=("arbitrary",), vmem_limit_bytes=_VMEM_LIMIT_BYTES),
        name="ffn",
    )(x, norm, w_in, w_out)


def _band_attention(i, qbuf, kbuf, vbuf, bias_ref, obuf, s_bufs, p_bufs):
    n_items = (TM // QH) * N_HEADS
    first_valid = jnp.where(i > 0, 0, TM)
    col = lax.broadcasted_iota(jnp.int32, (QB, KB), 1)

    def item(n):
        return n % N_HEADS, (n // N_HEADS) * QH

    def scores(n, s_ref):
        hh, r0 = item(n)
        s_ref[...] = _dot_nt(qbuf[hh, pl.ds(r0, QH), :], kbuf[hh, pl.ds(r0, KH), :])

    def softmax(n, s_ref, p_ref):
        hh, r0 = item(n)
        for a in range(QH // QB):
            sa = s_ref[a * QB:(a + 1) * QB, a * QB:a * QB + KB] + bias_ref[hh]
            sa = jnp.where(col >= first_valid - r0 - a * QB, sa, NEG_INF)
            p_ref[a * QB:(a + 1) * QB, a * QB:a * QB + KB] = jnp.exp2(
                sa - jnp.max(sa, axis=-1, keepdims=True)).astype(BF16)

    def values(n, p_ref):
        hh, r0 = item(n)
        o = _dot(p_ref[...], vbuf[hh, pl.ds(r0, KH), :])
        obuf[hh, pl.ds(r0, QH), :] = o[:, :HEAD_DIM] / o[:, HEAD_DIM:HEAD_DIM + 1]

    s0, s1 = s_bufs
    p0, p1 = p_bufs
    scores(0, s0)
    softmax(0, s0, p0)
    scores(1, s1)

    def group(k, s_cur, p_cur, s_nxt, p_prev):
        values(k - 1, p_prev)
        softmax(k, s_cur, p_cur)
        scores(k + 1, s_nxt)

    for k in range(1, n_items - 1):
        if k % 2:
            group(k, s1, p1, s0, p0)
        else:
            group(k, s0, p0, s1, p1)
    values(n_items - 2, p0)
    softmax(n_items - 1, s1, p1)
    values(n_items - 1, p1)


def _mix_prompt_kernel(x_ref, g_ref, win_ref, gb_ref, qg_ref, kg_ref, ones_ref, bias_ref,
                       pw_ref, ps_ref, cw_ref, cbias_ref, wa_ref, wp_ref, wc_ref, wo_ref,
                       y_ref, ko_ref, vo_ref, po_ref, co_ref,
                       qbuf, kbuf, vbuf, obuf, s0buf, s1buf, p0buf, p1buf, pext, cext):
    i = pl.program_id(0)

    @pl.when(i == 0)
    def _():
        kbuf[:, 0:TM, :] = jnp.zeros((N_HEADS, TM, HEAD_DIM), BF16)
        ones_col = lax.broadcasted_iota(jnp.int32, (N_HEADS, 2 * TM, VW), 2) == HEAD_DIM
        vbuf[...] = jnp.where(ones_col, 1.0, 0.0).astype(BF16)
        pext[0:_POOL_PAD, :] = jnp.zeros((_POOL_PAD, POOL_WIDTH), F32)
        cext[0:_CONV_PAD, :] = jnp.zeros((_CONV_PAD, CONV_WIDTH), F32)
        p0buf[...] = jnp.zeros((QH, KH), BF16)
        p1buf[...] = jnp.zeros((QH, KH), BF16)

    x = x_ref[...]
    h = _rms(x, g_ref[...]).astype(BF16)

    q = _dot(h, win_ref[:, _Q0:_Q0 + ATTN_WIDTH])
    k = _dot(h, win_ref[:, _K0:_K0 + ATTN_WIDTH])
    v = _dot(h, win_ref[:, _V0:_V0 + ATTN_WIDTH])
    qn = _head_rms(q, qg_ref[...], ones_ref) * (ATTN_SCALE * LOG2E)
    kn = _head_rms(k, kg_ref[...], ones_ref)
    ko_ref[...] = kn
    vo_ref[...] = v
    for hh in range(N_HEADS):
        sl = slice(hh * HEAD_DIM, (hh + 1) * HEAD_DIM)
        qbuf[hh] = qn[:, sl].astype(BF16)
        kbuf[hh, TM:2 * TM, :] = kn[:, sl].astype(BF16)
        vbuf[hh, TM:2 * TM, 0:HEAD_DIM] = v[:, sl].astype(BF16)

    pu = _dot(h, win_ref[:, _P0:_P0 + POOL_WIDTH])
    pext[_POOL_PAD:_POOL_PAD + TM, :] = pu
    d = _pool_delta(pext, pu, TM, i * TM)
    o_pool = _dot(d.astype(BF16), pw_ref[...]) * ps_ref[...]
    po_ref[...] = pext[_POOL_PAD + TM - POOL_HIST:_POOL_PAD + TM, :]
    pext[0:_POOL_PAD, :] = pext[TM:TM + _POOL_PAD, :]

    cu = _dot(h, win_ref[:, _CU0:_CU0 + CONV_WIDTH])
    cgate = _dot(h, win_ref[:, _CB0:_CB0 + CONV_WIDTH])
    cc = _dot(h, win_ref[:, _CC0:_CC0 + CONV_WIDTH])
    cext[_CONV_PAD:_CONV_PAD + TM, :] = cc * cu
    o_conv = cgate * _conv_taps(cext, TM, cw_ref, cbias_ref)
    co_ref[...] = cext[_CONV_PAD + TM - (CONV_K - 1):_CONV_PAD + TM, :]
    cext[0:_CONV_PAD, :] = cext[TM:TM + _CONV_PAD, :]

    side = _side_branches(h, o_pool, o_conv, win_ref, gb_ref, wp_ref, wc_ref)
    attn_gate = _gate(h, 0, win_ref, gb_ref)

    _band_attention(i, qbuf, kbuf, vbuf, bias_ref, obuf, (s0buf, s1buf), (p0buf, p1buf))

    o_attn = jnp.concatenate([obuf[hh] for hh in range(N_HEADS)], axis=1)
    y_ref[...] = _project_out(x, attn_gate, o_attn, side, wa_ref, wo_ref)

    kbuf[:, 0:TM, :] = kbuf[:, TM:2 * TM, :]
    vbuf[:, 0:TM, :] = vbuf[:, TM:2 * TM, :]


def _mix_prompt(x, layer, p):
    rows = x.shape[0]
    in_w = p["mix_w_in"].shape[-1]
    return pl.pallas_call(
        _mix_prompt_kernel,
        grid=(rows // TM,),
        in_specs=[
            pl.BlockSpec((TM, D_MODEL), lambda i: (i, 0)),
            _const_spec((1, D_MODEL), layer),
            _const_spec((D_MODEL, in_w), layer),
            _const_spec((1, 3 * D_MODEL), layer),
            _const_spec((1, ATTN_WIDTH), layer),
            _const_spec((1, ATTN_WIDTH), layer),
            _shared_spec((ATTN_WIDTH, ATTN_WIDTH)),
            _const_spec((N_HEADS, QB, KB), layer),
            _const_spec((POOL_WIDTH, POOL_WIDTH), layer),
            _const_spec((1, POOL_WIDTH), layer),
            _const_spec((CONV_K, CONV_WIDTH), layer),
            _const_spec((1, CONV_WIDTH), layer),
            _const_spec((ATTN_WIDTH, D_MODEL), layer),
            _const_spec((POOL_WIDTH, D_MODEL), layer),
            _const_spec((CONV_WIDTH, D_MODEL), layer),
            _const_spec((D_MODEL, D_MODEL), layer),
        ],
        out_specs=[
            pl.BlockSpec((TM, D_MODEL), lambda i: (i, 0)),
            pl.BlockSpec((TM, ATTN_WIDTH), lambda i: (0, 0)),
            pl.BlockSpec((TM, ATTN_WIDTH), lambda i: (0, 0)),
            pl.BlockSpec((POOL_HIST, POOL_WIDTH), lambda i: (0, 0)),
            pl.BlockSpec((CONV_K - 1, CONV_WIDTH), lambda i: (0, 0)),
        ],
        out_shape=[
            jax.ShapeDtypeStruct((rows, D_MODEL), F32),
            jax.ShapeDtypeStruct((TM, ATTN_WIDTH), F32),
            jax.ShapeDtypeStruct((TM, ATTN_WIDTH), F32),
            jax.ShapeDtypeStruct((POOL_HIST, POOL_WIDTH), F32),
            jax.ShapeDtypeStruct((CONV_K - 1, CONV_WIDTH), F32),
        ],
        scratch_shapes=[
            pltpu.VMEM((N_HEADS, TM, HEAD_DIM), BF16),
            pltpu.VMEM((N_HEADS, 2 * TM, HEAD_DIM), BF16),
            pltpu.VMEM((N_HEADS, 2 * TM, VW), BF16),
            pltpu.VMEM((N_HEADS, TM, HEAD_DIM), F32),
            pltpu.VMEM((QH, KH), F32),
            pltpu.VMEM((QH, KH), F32),
            pltpu.VMEM((QH, KH), BF16),
            pltpu.VMEM((QH, KH), BF16),
            pltpu.VMEM((_POOL_PAD + TM, POOL_WIDTH), F32),
            pltpu.VMEM((_CONV_PAD + TM, CONV_WIDTH), F32),
        ],
        compiler_params=pltpu.CompilerParams(
            dimension_semantics=("arbitrary",), vmem_limit_bytes=_VMEM_LIMIT_BYTES),
        name="mix_prompt",
    )(x, p["mix_norm"], p["mix_w_in"], p["gate_b"], p["q_norm"], p["k_norm"], p["head_ones"],
      p["bias_prompt"], p["pool_w"], p["pool_scale"], p["conv_w"], p["conv_b"],
      p["w_branch_attn"], p["w_branch_pool"], p["w_branch_conv"], p["w_out"])


def _mix_sample_kernel(t_new, x_ref, g_ref, win_ref, gb_ref, qg_ref, kg_ref, ones_ref,
                       b1_ref, b2_ref, hm_ref, kc_ref, vc_ref, sp_ref, sc_ref,
                       pw_ref, ps_ref, cw_ref, cbias_ref, wa_ref, wp_ref, wc_ref, wo_ref,
                       y_ref, ko_ref, vo_ref, po_ref, co_ref,
                       hbuf, qbuf, knbuf, vnbuf, obuf, pubuf, dbuf, cinbuf, ybuf, pext, cext):
    b = pl.program_id(0)
    T = t_new

    @pl.when(b == 0)
    def _():
        h = _rms(x_ref[...], g_ref[...]).astype(BF16)
        hbuf[...] = h
        q = _dot(h, win_ref[:, _Q0:_Q0 + ATTN_WIDTH])
        k = _dot(h, win_ref[:, _K0:_K0 + ATTN_WIDTH])
        v = _dot(h, win_ref[:, _V0:_V0 + ATTN_WIDTH])
        qn = _head_rms(q, qg_ref[...], ones_ref) * ATTN_SCALE
        kn = _head_rms(k, kg_ref[...], ones_ref)
        ko_ref[...] = kn
        vo_ref[...] = v
        qbuf[...] = qn.astype(BF16)
        knbuf[...] = kn.astype(BF16)
        vnbuf[...] = v.astype(BF16)
        pubuf[...] = _dot(h, win_ref[:, _P0:_P0 + POOL_WIDTH])
        cu = _dot(h, win_ref[:, _CU0:_CU0 + CONV_WIDTH])
        cc = _dot(h, win_ref[:, _CC0:_CC0 + CONV_WIDTH])
        cinbuf[...] = cc * cu

    r0 = pl.multiple_of(b * T, T)

    q_b = qbuf[pl.ds(r0, T), :]
    hm = hm_ref[...]
    qexp = jnp.where(hm > 0, jnp.concatenate([q_b] * N_HEADS, axis=0), jnp.zeros((), BF16))
    kc = kc_ref[...].reshape(kc_ref.shape[0], ATTN_WIDTH).astype(BF16)
    vc = vc_ref[...].reshape(vc_ref.shape[0], ATTN_WIDTH).astype(BF16)
    s1 = _dot_nt(qexp, kc) + b1_ref[...]
    s2 = _dot_nt(qexp, knbuf[pl.ds(r0, T), :]) + b2_ref[...]
    m = jnp.maximum(jnp.max(s1, axis=-1, keepdims=True), jnp.max(s2, axis=-1, keepdims=True))
    p1 = jnp.exp(s1 - m)
    p2 = jnp.exp(s2 - m)
    denom = jnp.sum(p1, axis=-1, keepdims=True) + jnp.sum(p2, axis=-1, keepdims=True)
    oall = (_dot(p1.astype(BF16), vc) + _dot(p2.astype(BF16), vnbuf[pl.ds(r0, T), :])) / denom
    oall = oall * hm
    o = oall[0:T, :]
    for hh in range(1, N_HEADS):
        o = o + oall[hh * T:(hh + 1) * T, :]
    obuf[pl.ds(r0, T), :] = o

    pu = pubuf[pl.ds(r0, T), :]
    pext[_POOL_PAD - POOL_HIST:_POOL_PAD, :] = sp_ref[b]
    pext[_POOL_PAD:_POOL_PAD + T, :] = pu
    dbuf[pl.ds(r0, T), :] = _pool_delta(pext, pu, T, PAST_LEN)
    po_ref[b] = pext[_POOL_PAD + T - POOL_HIST:_POOL_PAD + T, :]

    cext[_CONV_PAD - (CONV_K - 1):_CONV_PAD, :] = sc_ref[b]
    cext[_CONV_PAD:_CONV_PAD + T, :] = cinbuf[pl.ds(r0, T), :]
    ybuf[pl.ds(r0, T), :] = _conv_taps(cext, T, cw_ref, cbias_ref)
    co_ref[b] = cext[_CONV_PAD + T - (CONV_K - 1):_CONV_PAD + T, :]

    @pl.when(b == pl.num_programs(0) - 1)
    def _():
        h = hbuf[...]
        o_pool = _dot(dbuf[...].astype(BF16), pw_ref[...]) * ps_ref[...]
        o_conv = _dot(h, win_ref[:, _CB0:_CB0 + CONV_WIDTH]) * ybuf[...]
        side = _side_branches(h, o_pool, o_conv, win_ref, gb_ref, wp_ref, wc_ref)
        y_ref[...] = _project_out(x_ref[...], _gate(h, 0, win_ref, gb_ref), obuf[...], side,
                                  wa_ref, wo_ref)


def _mix_sample(x, cache_k, cache_v, state_pool, state_conv, layer, p):
    rows = x.shape[0]
    n_streams, r_cache = cache_k.shape[1], cache_k.shape[2]
    t_new = rows // n_streams
    in_w = p["mix_w_in"].shape[-1]
    cache_spec = pl.BlockSpec((None, None, r_cache, N_HEADS, HEAD_DIM),
                              lambda b: (layer, b, 0, 0, 0))
    full = lambda shape: pl.BlockSpec(tuple(shape), lambda b: (0,) * len(shape))
    return pl.pallas_call(
        functools.partial(_mix_sample_kernel, t_new),
        grid=(n_streams,),
        in_specs=[
            _shared_spec((rows, D_MODEL)),
            _const_spec((1, D_MODEL), layer),
            _const_spec((D_MODEL, in_w), layer),
            _const_spec((1, 3 * D_MODEL), layer),
            _const_spec((1, ATTN_WIDTH), layer),
            _const_spec((1, ATTN_WIDTH), layer),
            _shared_spec((ATTN_WIDTH, ATTN_WIDTH)),
            _const_spec((N_HEADS * t_new, r_cache), layer),
            _const_spec((N_HEADS * t_new, t_new), layer),
            _shared_spec((N_HEADS * t_new, ATTN_WIDTH)),
            cache_spec,
            cache_spec,
            _const_spec((n_streams, POOL_HIST, POOL_WIDTH), layer),
            _const_spec((n_streams, CONV_K - 1, CONV_WIDTH), layer),
            _const_spec((POOL_WIDTH, POOL_WIDTH), layer),
            _const_spec((1, POOL_WIDTH), layer),
            _const_spec((CONV_K, CONV_WIDTH), layer),
            _const_spec((1, CONV_WIDTH), layer),
            _const_spec((ATTN_WIDTH, D_MODEL), layer),
            _const_spec((POOL_WIDTH, D_MODEL), layer),
            _const_spec((CONV_WIDTH, D_MODEL), layer),
            _const_spec((D_MODEL, D_MODEL), layer),
        ],
        out_specs=[
            full((rows, D_MODEL)),
            full((rows, ATTN_WIDTH)),
            full((rows, ATTN_WIDTH)),
            full((n_streams, POOL_HIST, POOL_WIDTH)),
            full((n_streams, CONV_K - 1, CONV_WIDTH)),
        ],
        out_shape=[
            jax.ShapeDtypeStruct((rows, D_MODEL), F32),
            jax.ShapeDtypeStruct((rows, ATTN_WIDTH), F32),
            jax.ShapeDtypeStruct((rows, ATTN_WIDTH), F32),
            jax.ShapeDtypeStruct((n_streams, POOL_HIST, POOL_WIDTH), F32),
            jax.ShapeDtypeStruct((n_streams, CONV_K - 1, CONV_WIDTH), F32),
        ],
        scratch_shapes=[
            pltpu.VMEM((rows, D_MODEL), BF16),
            pltpu.VMEM((rows, ATTN_WIDTH), BF16),
            pltpu.VMEM((rows, ATTN_WIDTH), BF16),
            pltpu.VMEM((rows, ATTN_WIDTH), BF16),
            pltpu.VMEM((rows, ATTN_WIDTH), F32),
            pltpu.VMEM((rows, POOL_WIDTH), F32),
            pltpu.VMEM((rows, POOL_WIDTH), F32),
            pltpu.VMEM((rows, CONV_WIDTH), F32),
            pltpu.VMEM((rows, CONV_WIDTH), F32),
            pltpu.VMEM((_POOL_PAD + t_new, POOL_WIDTH), F32),
            pltpu.VMEM((_CONV_PAD + t_new, CONV_WIDTH), F32),
        ],
        compiler_params=pltpu.CompilerParams(
            dimension_semantics=("arbitrary",), vmem_limit_bytes=_VMEM_LIMIT_BYTES),
        name="mix_sample",
    )(x, p["mix_norm"], p["mix_w_in"], p["gate_b"], p["q_norm"], p["k_norm"], p["head_ones"],
      p["bias_cache"], p["bias_new"], p["head_mask"], cache_k, cache_v, state_pool, state_conv,
      p["pool_w"], p["pool_scale"], p["conv_w"], p["conv_b"],
      p["w_branch_attn"], p["w_branch_pool"], p["w_branch_conv"], p["w_out"])


def _rel_lookup(rel_bias, lo, hi):
    lead = rel_bias.shape[:-1]
    n_left = max(0, min(hi, -REL_CLIP) - lo)
    n_right = max(0, hi - max(lo, REL_CLIP + 1))
    a, b = max(lo, -REL_CLIP), min(hi, REL_CLIP + 1)
    parts = []
    if n_left:
        parts.append(jnp.broadcast_to(rel_bias[..., :1], lead + (n_left,)))
    if b > a:
        parts.append(rel_bias[..., a + REL_CLIP:b + REL_CLIP])
    if n_right:
        parts.append(jnp.broadcast_to(rel_bias[..., -1:], lead + (n_right,)))
    return jnp.concatenate(parts, axis=-1)


def _toeplitz(g, rows):
    n = g.shape[-1]
    lead = g.shape[:-1]
    t = jnp.broadcast_to(g[..., None, :], lead + (rows, n)).reshape(lead + (rows * n,))
    return t[..., :rows * (n - 1)].reshape(lead + (rows, n - 1))


def _band_bias(rel_bias, n_query, n_past, n_key):
    g = jnp.concatenate([_rel_lookup(rel_bias, -n_past, n_key - n_past),
                         _rel_lookup(rel_bias, -n_past - n_query, -n_past)], axis=-1)
    return _toeplitz(g, n_query)[..., :n_key].astype(F32)


def _prompt_bias(rel_bias):
    band = _band_bias(rel_bias, CHUNK, N_PREV_CHUNKS * CHUNK, BAND) * LOG2E
    masked = jnp.full(band.shape[:-1] + (KB - BAND,), NEG_INF, F32)
    return jnp.concatenate([jnp.concatenate([band, masked], axis=-1),
                            jnp.concatenate([masked, band], axis=-1)], axis=-2)


def _sample_bias(rel_bias, t_new, r_cache):
    depth = rel_bias.shape[0]
    table = _band_bias(rel_bias, t_new, r_cache, r_cache + t_new)
    table = table.reshape(depth, N_HEADS * t_new, r_cache + t_new)
    return table[..., :r_cache], table[..., r_cache:]


def kernel(x_prompt, x_sample, cache_attn_k, cache_attn_v, state_pool, state_conv, ffn1_norm, ffn1_w_in, ffn1_w_out, mix_norm, mix_w_in, gate_b, q_norm, k_norm, rel_bias, pool_w, pool_scale, conv_w, conv_b, w_branch_attn, w_branch_pool, w_branch_conv, w_out, ffn2_norm, ffn2_w_in, ffn2_w_out):
    batch, seq, _ = x_prompt.shape
    n_streams, t_new, _ = x_sample.shape
    depth = mix_w_in.shape[0]
    r_cache = cache_attn_k.shape[2]
    assert batch == 1 and seq % TM == 0 and seq >= TM
    assert r_cache == N_PREV_CHUNKS * CHUNK and PAST_LEN % CHUNK == 0 and t_new <= CHUNK

    row = lambda a: a.reshape(depth, 1, a.shape[-1])
    head = np.arange(ATTN_WIDTH) // HEAD_DIM
    n_groups = len(POOL_WINDOWS)
    pool_bd = jnp.einsum("lgcd,gh->lgchd", pool_w, jnp.eye(n_groups, dtype=pool_w.dtype))
    bias_cache, bias_new = _sample_bias(rel_bias, t_new, r_cache)
    p = dict(
        mix_norm=row(mix_norm), mix_w_in=mix_w_in.astype(BF16), gate_b=row(gate_b),
        q_norm=row(jnp.tile(q_norm, (1, N_HEADS))), k_norm=row(jnp.tile(k_norm, (1, N_HEADS))),
        head_ones=jnp.asarray(head[:, None] == head[None, :], BF16),
        bias_prompt=_prompt_bias(rel_bias), bias_cache=bias_cache, bias_new=bias_new,
        head_mask=jnp.asarray(np.repeat(np.arange(N_HEADS), t_new)[:, None] == head[None, :], F32),
        pool_w=pool_bd.reshape(depth, POOL_WIDTH, POOL_WIDTH).astype(BF16),
        pool_scale=row(pool_scale), conv_w=conv_w, conv_b=row(conv_b),
        w_branch_attn=w_branch_attn.astype(BF16), w_branch_pool=w_branch_pool.astype(BF16),
        w_branch_conv=w_branch_conv.astype(BF16), w_out=w_out.astype(BF16),
    )
    f1 = (row(ffn1_norm), ffn1_w_in.astype(BF16), ffn1_w_out.astype(BF16))
    f2 = (row(ffn2_norm), ffn2_w_in.astype(BF16), ffn2_w_out.astype(BF16))

    xp = x_prompt.reshape(seq, D_MODEL)
    xs = x_sample.reshape(n_streams * t_new, D_MODEL)
    outs = [[] for _ in range(8)]
    for l in range(depth):
        xp = _ffn(xp, *f1, l)
        xs = _ffn(xs, *f1, l)
        xp, kp, vp, pp, cp = _mix_prompt(xp, l, p)
        xs, ks, vs, ps, cs = _mix_sample(xs, cache_attn_k, cache_attn_v, state_pool, state_conv, l, p)
        xp = _ffn(xp, *f2, l)
        xs = _ffn(xs, *f2, l)
        for acc, val in zip(outs, (kp, vp, pp, cp, ks, vs, ps, cs)):
            acc.append(val)
    kp, vp, pp, cp, ks, vs, ps, cs = (jnp.stack(o) for o in outs)
    r_keep = min(N_PREV_CHUNKS * CHUNK, seq)
    return (xp.reshape(batch, seq, D_MODEL), xs.reshape(n_streams, t_new, D_MODEL),
            kp.reshape(depth, batch, r_keep, N_HEADS, HEAD_DIM),
            vp.reshape(depth, batch, r_keep, N_HEADS, HEAD_DIM),
            pp.reshape(depth, batch, POOL_HIST, POOL_WIDTH),
            cp.reshape(depth, batch, CONV_K - 1, CONV_WIDTH),
            ks.reshape(depth, n_streams, t_new, N_HEADS, HEAD_DIM),
            vs.reshape(depth, n_streams, t_new, N_HEADS, HEAD_DIM),
            ps, cs)
```

```python
import functools

import jax
import jax.numpy as jnp
import numpy as np
from jax import lax
from jax.experimental import pallas as pl
from jax.experimental.pallas import tpu as pltpu

D_MODEL = 1024
DEPTH = 4
PAST_LEN = 1024
CHUNK = 64
N_PREV_CHUNKS = 8
BAND = (N_PREV_CHUNKS + 1) * CHUNK
N_HEADS = 8
HEAD_DIM = 64
ATTN_WIDTH = N_HEADS * HEAD_DIM
ATTN_SCALE = HEAD_DIM ** -0.5
REL_CLIP = 128
POOL_WINDOWS = (2, 4, 8, 16)
POOL_GROUP = 64
POOL_WIDTH = len(POOL_WINDOWS) * POOL_GROUP
POOL_HIST = max(POOL_WINDOWS) - 1
CONV_WIDTH = 256
CONV_K = 3
D_FF = 2816
EPS = 1e-6
NEG_INF = -1e30

_Q0, _K0, _V0 = 0, ATTN_WIDTH, 2 * ATTN_WIDTH
_P0 = 3 * ATTN_WIDTH
_CU0 = _P0 + POOL_WIDTH
_CB0 = _CU0 + CONV_WIDTH
_CC0 = _CB0 + CONV_WIDTH
_G0 = _CC0 + CONV_WIDTH

TM = N_PREV_CHUNKS * CHUNK
QB = 2 * CHUNK
KB = BAND + CHUNK
QH = 2 * QB
KH = KB + QB
VW = 2 * HEAD_DIM
LOG2E = 1.4426950408889634
_FF_CHUNKS = ((0, 1024), (1024, 2048), (2048, D_FF))
_FFN_TM = 2 * TM
_POOL_PAD = 16
_CONV_PAD = 8

_V7X_VMEM_BYTES = 64 * 2 ** 20
_VMEM_LIMIT_BYTES = _V7X_VMEM_BYTES - 8 * 2 ** 20

F32 = jnp.float32
BF16 = jnp.bfloat16


def _dot(a, b):
    return jnp.dot(a, b, preferred_element_type=F32)


def _dot_nt(a, b):
    return lax.dot_general(a, b, (((1,), (1,)), ((), ())), preferred_element_type=F32)


def _rms(x, g):
    ms = jnp.mean(x * x, axis=-1, keepdims=True)
    return x * lax.rsqrt(ms + EPS) * g


def _head_rms(q, g, head_ones_ref):
    ss = _dot((q * q).astype(BF16), head_ones_ref[...])
    return q * lax.rsqrt(ss * (1.0 / HEAD_DIM) + EPS) * g


def _pool_delta(ext_ref, pu, rows, pos0):
    def back(j):
        return ext_ref[_POOL_PAD - j:_POOL_PAD - j + rows, :]
    s2 = pu + back(1)
    s4 = s2 + back(2) + back(3)
    s8 = s4 + back(4) + back(5) + back(6) + back(7)
    s16 = s8
    for j in range(8, 16):
        s16 = s16 + back(j)
    lane = lax.broadcasted_iota(jnp.int32, (rows, POOL_WIDTH), 1)
    pos = lax.broadcasted_iota(jnp.int32, (rows, POOL_WIDTH), 0) + pos0
    g0, g1, g2 = lane < POOL_GROUP, lane < 2 * POOL_GROUP, lane < 3 * POOL_GROUP
    wsum = jnp.where(g0, s2, jnp.where(g1, s4, jnp.where(g2, s8, s16)))
    wlen = jnp.where(g0, 2, jnp.where(g1, 4, jnp.where(g2, 8, 16)))
    cnt = jnp.minimum(wlen, pos + 1).astype(F32)
    return wsum / cnt - pu


def _conv_taps(ext_ref, rows, cw_ref, cbias_ref):
    y = cbias_ref[...]
    for t in range(CONV_K):
        off = _CONV_PAD - (CONV_K - 1) + t
        y = y + ext_ref[off:off + rows, :] * cw_ref[t:t + 1, :]
    return y


def _gate(h, idx, win_ref, gb_ref):
    lo, hi = _G0 + idx * D_MODEL, _G0 + (idx + 1) * D_MODEL
    return jax.nn.sigmoid(_dot(h, win_ref[:, lo:hi]) + gb_ref[:, idx * D_MODEL:(idx + 1) * D_MODEL])


def _side_branches(h, o_pool, o_conv, win_ref, gb_ref, wp_ref, wc_ref):
    return (_gate(h, 1, win_ref, gb_ref) * _dot(o_pool.astype(BF16), wp_ref[...])
            + _gate(h, 2, win_ref, gb_ref) * _dot(o_conv.astype(BF16), wc_ref[...]))


def _project_out(x, attn_gate, o_attn, side, wa_ref, wo_ref):
    m = attn_gate * _dot(o_attn.astype(BF16), wa_ref[...]) + side
    return x + _dot(m.astype(BF16), wo_ref[...])


def _ffn_kernel(x_ref, g_ref, win_ref, wout_ref, o_ref):
    x = x_ref[...]
    h = _rms(x, g_ref[...]).astype(BF16)
    acc = None
    for a, b in _FF_CHUNKS:
        gate = _dot(h, win_ref[:, a:b])
        up = _dot(h, win_ref[:, D_FF + a:D_FF + b])
        act = (gate * jax.nn.sigmoid(gate) * up).astype(BF16)
        part = _dot(act, wout_ref[a:b, :])
        acc = part if acc is None else acc + part
    o_ref[...] = x + 0.5 * acc


def _const_spec(shape, layer):
    nd = len(shape)
    return pl.BlockSpec((None,) + tuple(shape), lambda i: (layer,) + (0,) * nd,
                        pipeline_mode=pl.Buffered(1))


def _shared_spec(shape):
    nd = len(shape)
    return pl.BlockSpec(tuple(shape), lambda i: (0,) * nd, pipeline_mode=pl.Buffered(1))


def _ffn(x, norm, w_in, w_out, layer):
    rows = x.shape[0]
    tm = min(_FFN_TM, rows)
    return pl.pallas_call(
        _ffn_kernel,
        grid=(rows // tm,),
        in_specs=[
            pl.BlockSpec((tm, D_MODEL), lambda i: (i, 0)),
            _const_spec((1, D_MODEL), layer),
            _const_spec((D_MODEL, 2 * D_FF), layer),
            _const_spec((D_FF, D_MODEL), layer),
        ],
        out_specs=pl.BlockSpec((tm, D_MODEL), lambda i: (i, 0)),
        out_shape=jax.ShapeDtypeStruct((rows, D_MODEL), F32),
        compiler_params=pltpu.CompilerParams(
            dimension_semantics=("arbitrary",), vmem_limit_bytes=_VMEM_LIMIT_BYTES),
        name="ffn",
    )(x, norm, w_in, w_out)


def _band_attention(i, qbuf, kbuf, vbuf, bias_ref, obuf, s_bufs, p_bufs):
    n_items = (TM // QH) * N_HEADS
    first_valid = jnp.where(i > 0, 0, TM)
    col = lax.broadcasted_iota(jnp.int32, (QB, KB), 1)

    def item(n):
        return n % N_HEADS, (n // N_HEADS) * QH

    def scores(n, s_ref):
        hh, r0 = item(n)
        s_ref[...] = _dot_nt(qbuf[hh, pl.ds(r0, QH), :], kbuf[hh, pl.ds(r0, KH), :])

    def softmax(n, s_ref, p_ref):
        hh, r0 = item(n)
        for a in range(QH // QB):
            sa = s_ref[a * QB:(a + 1) * QB, a * QB:a * QB + KB] + bias_ref[hh]
            sa = jnp.where(col >= first_valid - r0 - a * QB, sa, NEG_INF)
            p_ref[a * QB:(a + 1) * QB, a * QB:a * QB + KB] = jnp.exp2(
                sa - jnp.max(sa, axis=-1, keepdims=True)).astype(BF16)

    def values(n, p_ref):
        hh, r0 = item(n)
        o = _dot(p_ref[...], vbuf[hh, pl.ds(r0, KH), :])
        obuf[hh, pl.ds(r0, QH), :] = o[:, :HEAD_DIM] / o[:, HEAD_DIM:HEAD_DIM + 1]

    s0, s1 = s_bufs
    p0, p1 = p_bufs
    scores(0, s0)
    softmax(0, s0, p0)
    scores(1, s1)

    def group(k, s_cur, p_cur, s_nxt, p_prev):
        values(k - 1, p_prev)
        softmax(k, s_cur, p_cur)
        scores(k + 1, s_nxt)

    for k in range(1, n_items - 1):
        if k % 2:
            group(k, s1, p1, s0, p0)
        else:
            group(k, s0, p0, s1, p1)
    values(n_items - 2, p0)
    softmax(n_items - 1, s1, p1)
    values(n_items - 1, p1)


def _mix_prompt_kernel(x_ref, g_ref, win_ref, gb_ref, qg_ref, kg_ref, ones_ref, bias_ref,
                       pw_ref, ps_ref, cw_ref, cbias_ref, wa_ref, wp_ref, wc_ref, wo_ref,
                       y_ref, ko_ref, vo_ref, po_ref, co_ref,
                       qbuf, kbuf, vbuf, obuf, s0buf, s1buf, p0buf, p1buf, pext, cext):
    i = pl.program_id(0)

    @pl.when(i == 0)
    def _():
        kbuf[:, 0:TM, :] = jnp.zeros((N_HEADS, TM, HEAD_DIM), BF16)
        ones_col = lax.broadcasted_iota(jnp.int32, (N_HEADS, 2 * TM, VW), 2) == HEAD_DIM
        vbuf[...] = jnp.where(ones_col, 1.0, 0.0).astype(BF16)
        pext[0:_POOL_PAD, :] = jnp.zeros((_POOL_PAD, POOL_WIDTH), F32)
        cext[0:_CONV_PAD, :] = jnp.zeros((_CONV_PAD, CONV_WIDTH), F32)
        p0buf[...] = jnp.zeros((QH, KH), BF16)
        p1buf[...] = jnp.zeros((QH, KH), BF16)

    x = x_ref[...]
    h = _rms(x, g_ref[...]).astype(BF16)

    q = _dot(h, win_ref[:, _Q0:_Q0 + ATTN_WIDTH])
    k = _dot(h, win_ref[:, _K0:_K0 + ATTN_WIDTH])
    v = _dot(h, win_ref[:, _V0:_V0 + ATTN_WIDTH])
    qn = _head_rms(q, qg_ref[...], ones_ref) * (ATTN_SCALE * LOG2E)
    kn = _head_rms(k, kg_ref[...], ones_ref)
    ko_ref[...] = kn
    vo_ref[...] = v
    for hh in range(N_HEADS):
        sl = slice(hh * HEAD_DIM, (hh + 1) * HEAD_DIM)
        qbuf[hh] = qn[:, sl].astype(BF16)
        kbuf[hh, TM:2 * TM, :] = kn[:, sl].astype(BF16)
        vbuf[hh, TM:2 * TM, 0:HEAD_DIM] = v[:, sl].astype(BF16)

    pu = _dot(h, win_ref[:, _P0:_P0 + POOL_WIDTH])
    pext[_POOL_PAD:_POOL_PAD + TM, :] = pu
    d = _pool_delta(pext, pu, TM, i * TM)
    o_pool = _dot(d.astype(BF16), pw_ref[...]) * ps_ref[...]
    po_ref[...] = pext[_POOL_PAD + TM - POOL_HIST:_POOL_PAD + TM, :]
    pext[0:_POOL_PAD, :] = pext[TM:TM + _POOL_PAD, :]

    cu = _dot(h, win_ref[:, _CU0:_CU0 + CONV_WIDTH])
    cgate = _dot(h, win_ref[:, _CB0:_CB0 + CONV_WIDTH])
    cc = _dot(h, win_ref[:, _CC0:_CC0 + CONV_WIDTH])
    cext[_CONV_PAD:_CONV_PAD + TM, :] = cc * cu
    o_conv = cgate * _conv_taps(cext, TM, cw_ref, cbias_ref)
    co_ref[...] = cext[_CONV_PAD + TM - (CONV_K - 1):_CONV_PAD + TM, :]
    cext[0:_CONV_PAD, :] = cext[TM:TM + _CONV_PAD, :]

    side = _side_branches(h, o_pool, o_conv, win_ref, gb_ref, wp_ref, wc_ref)
    attn_gate = _gate(h, 0, win_ref, gb_ref)

    _band_attention(i, qbuf, kbuf, vbuf, bias_ref, obuf, (s0buf, s1buf), (p0buf, p1buf))

    o_attn = jnp.concatenate([obuf[hh] for hh in range(N_HEADS)], axis=1)
    y_ref[...] = _project_out(x, attn_gate, o_attn, side, wa_ref, wo_ref)

    kbuf[:, 0:TM, :] = kbuf[:, TM:2 * TM, :]
    vbuf[:, 0:TM, :] = vbuf[:, TM:2 * TM, :]


def _mix_prompt(x, layer, p):
    rows = x.shape[0]
    in_w = p["mix_w_in"].shape[-1]
    return pl.pallas_call(
        _mix_prompt_kernel,
        grid=(rows // TM,),
        in_specs=[
            pl.BlockSpec((TM, D_MODEL), lambda i: (i, 0)),
            _const_spec((1, D_MODEL), layer),
            _const_spec((D_MODEL, in_w), layer),
            _const_spec((1, 3 * D_MODEL), layer),
            _const_spec((1, ATTN_WIDTH), layer),
            _const_spec((1, ATTN_WIDTH), layer),
            _shared_spec((ATTN_WIDTH, ATTN_WIDTH)),
            _const_spec((N_HEADS, QB, KB), layer),
            _const_spec((POOL_WIDTH, POOL_WIDTH), layer),
            _const_spec((1, POOL_WIDTH), layer),
            _const_spec((CONV_K, CONV_WIDTH), layer),
            _const_spec((1, CONV_WIDTH), layer),
            _const_spec((ATTN_WIDTH, D_MODEL), layer),
            _const_spec((POOL_WIDTH, D_MODEL), layer),
            _const_spec((CONV_WIDTH, D_MODEL), layer),
            _const_spec((D_MODEL, D_MODEL), layer),
        ],
        out_specs=[
            pl.BlockSpec((TM, D_MODEL), lambda i: (i, 0)),
            pl.BlockSpec((TM, ATTN_WIDTH), lambda i: (0, 0)),
            pl.BlockSpec((TM, ATTN_WIDTH), lambda i: (0, 0)),
            pl.BlockSpec((POOL_HIST, POOL_WIDTH), lambda i: (0, 0)),
            pl.BlockSpec((CONV_K - 1, CONV_WIDTH), lambda i: (0, 0)),
        ],
        out_shape=[
            jax.ShapeDtypeStruct((rows, D_MODEL), F32),
            jax.ShapeDtypeStruct((TM, ATTN_WIDTH), F32),
            jax.ShapeDtypeStruct((TM, ATTN_WIDTH), F32),
            jax.ShapeDtypeStruct((POOL_HIST, POOL_WIDTH), F32),
            jax.ShapeDtypeStruct((CONV_K - 1, CONV_WIDTH), F32),
        ],
        scratch_shapes=[
            pltpu.VMEM((N_HEADS, TM, HEAD_DIM), BF16),
            pltpu.VMEM((N_HEADS, 2 * TM, HEAD_DIM), BF16),
            pltpu.VMEM((N_HEADS, 2 * TM, VW), BF16),
            pltpu.VMEM((N_HEADS, TM, HEAD_DIM), F32),
            pltpu.VMEM((QH, KH), F32),
            pltpu.VMEM((QH, KH), F32),
            pltpu.VMEM((QH, KH), BF16),
            pltpu.VMEM((QH, KH), BF16),
            pltpu.VMEM((_POOL_PAD + TM, POOL_WIDTH), F32),
            pltpu.VMEM((_CONV_PAD + TM, CONV_WIDTH), F32),
        ],
        compiler_params=pltpu.CompilerParams(
            dimension_semantics=("arbitrary",), vmem_limit_bytes=_VMEM_LIMIT_BYTES),
        name="mix_prompt",
    )(x, p["mix_norm"], p["mix_w_in"], p["gate_b"], p["q_norm"], p["k_norm"], p["head_ones"],
      p["bias_prompt"], p["pool_w"], p["pool_scale"], p["conv_w"], p["conv_b"],
      p["w_branch_attn"], p["w_branch_pool"], p["w_branch_conv"], p["w_out"])


def _mix_sample_kernel(t_new, x_ref, g_ref, win_ref, gb_ref, qg_ref, kg_ref, ones_ref,
                       b1_ref, b2_ref, hm_ref, kct_ref, vct_ref, sp_ref, sc_ref,
                       pw_ref, ps_ref, cw_ref, cbias_ref, wa_ref, wp_ref, wc_ref, wo_ref,
                       y_ref, ko_ref, vo_ref, po_ref, co_ref,
                       hbuf, qbuf, knbuf, vnbuf, obuf, pubuf, dbuf, cinbuf, ybuf, pext, cext):
    b = pl.program_id(0)
    T = t_new

    @pl.when(b == 0)
    def _():
        h = _rms(x_ref[...], g_ref[...]).astype(BF16)
        hbuf[...] = h
        q = _dot(h, win_ref[:, _Q0:_Q0 + ATTN_WIDTH])
        k = _dot(h, win_ref[:, _K0:_K0 + ATTN_WIDTH])
        v = _dot(h, win_ref[:, _V0:_V0 + ATTN_WIDTH])
        qn = _head_rms(q, qg_ref[...], ones_ref) * ATTN_SCALE
        kn = _head_rms(k, kg_ref[...], ones_ref)
        ko_ref[...] = kn
        vo_ref[...] = v
        qbuf[...] = qn.astype(BF16)
        knbuf[...] = kn.astype(BF16)
        vnbuf[...] = v.astype(BF16)
        pubuf[...] = _dot(h, win_ref[:, _P0:_P0 + POOL_WIDTH])
        cu = _dot(h, win_ref[:, _CU0:_CU0 + CONV_WIDTH])
        cc = _dot(h, win_ref[:, _CC0:_CC0 + CONV_WIDTH])
        cinbuf[...] = cc * cu

    r0 = pl.multiple_of(b * T, T)

    q_b = qbuf[pl.ds(r0, T), :]
    hm = hm_ref[...]
    qexp = jnp.where(hm > 0, jnp.concatenate([q_b] * N_HEADS, axis=0), jnp.zeros((), BF16))
    kct = kct_ref[...].astype(BF16)
    vct = vct_ref[...].astype(BF16)
    s1 = _dot(qexp, kct) + b1_ref[...]
    s2 = _dot_nt(qexp, knbuf[pl.ds(r0, T), :]) + b2_ref[...]
    m = jnp.maximum(jnp.max(s1, axis=-1, keepdims=True), jnp.max(s2, axis=-1, keepdims=True))
    p1 = jnp.exp(s1 - m)
    p2 = jnp.exp(s2 - m)
    denom = jnp.sum(p1, axis=-1, keepdims=True) + jnp.sum(p2, axis=-1, keepdims=True)
    oall = (_dot_nt(p1.astype(BF16), vct) + _dot(p2.astype(BF16), vnbuf[pl.ds(r0, T), :])) / denom
    oall = oall * hm
    o = oall[0:T, :]
    for hh in range(1, N_HEADS):
        o = o + oall[hh * T:(hh + 1) * T, :]
    obuf[pl.ds(r0, T), :] = o

    pu = pubuf[pl.ds(r0, T), :]
    pext[_POOL_PAD - POOL_HIST:_POOL_PAD, :] = sp_ref[b]
    pext[_POOL_PAD:_POOL_PAD + T, :] = pu
    dbuf[pl.ds(r0, T), :] = _pool_delta(pext, pu, T, PAST_LEN)
    po_ref[b] = pext[_POOL_PAD + T - POOL_HIST:_POOL_PAD + T, :]

    cext[_CONV_PAD - (CONV_K - 1):_CONV_PAD, :] = sc_ref[b]
    cext[_CONV_PAD:_CONV_PAD + T, :] = cinbuf[pl.ds(r0, T), :]
    ybuf[pl.ds(r0, T), :] = _conv_taps(cext, T, cw_ref, cbias_ref)
    co_ref[b] = cext[_CONV_PAD + T - (CONV_K - 1):_CONV_PAD + T, :]

    @pl.when(b == pl.num_programs(0) - 1)
    def _():
        h = hbuf[...]
        o_pool = _dot(dbuf[...].astype(BF16), pw_ref[...]) * ps_ref[...]
        o_conv = _dot(h, win_ref[:, _CB0:_CB0 + CONV_WIDTH]) * ybuf[...]
        side = _side_branches(h, o_pool, o_conv, win_ref, gb_ref, wp_ref, wc_ref)
        y_ref[...] = _project_out(x_ref[...], _gate(h, 0, win_ref, gb_ref), obuf[...], side,
                                  wa_ref, wo_ref)


def _mix_sample(x, cache_kt, cache_vt, state_pool, state_conv, layer, p):
    rows = x.shape[0]
    n_streams, r_cache = cache_kt.shape[1], cache_kt.shape[3]
    t_new = rows // n_streams
    in_w = p["mix_w_in"].shape[-1]
    cache_spec = pl.BlockSpec((None, None, ATTN_WIDTH, r_cache), lambda b: (layer, b, 0, 0))
    full = lambda shape: pl.BlockSpec(tuple(shape), lambda b: (0,) * len(shape))
    return pl.pallas_call(
        functools.partial(_mix_sample_kernel, t_new),
        grid=(n_streams,),
        in_specs=[
            _shared_spec((rows, D_MODEL)),
            _const_spec((1, D_MODEL), layer),
            _const_spec((D_MODEL, in_w), layer),
            _const_spec((1, 3 * D_MODEL), layer),
            _const_spec((1, ATTN_WIDTH), layer),
            _const_spec((1, ATTN_WIDTH), layer),
            _shared_spec((ATTN_WIDTH, ATTN_WIDTH)),
            _const_spec((N_HEADS * t_new, r_cache), layer),
            _const_spec((N_HEADS * t_new, t_new), layer),
            _shared_spec((N_HEADS * t_new, ATTN_WIDTH)),
            cache_spec,
            cache_spec,
            _const_spec((n_streams, POOL_HIST, POOL_WIDTH), layer),
            _const_spec((n_streams, CONV_K - 1, CONV_WIDTH), layer),
            _const_spec((POOL_WIDTH, POOL_WIDTH), layer),
            _const_spec((1, POOL_WIDTH), layer),
            _const_spec((CONV_K, CONV_WIDTH), layer),
            _const_spec((1, CONV_WIDTH), layer),
            _const_spec((ATTN_WIDTH, D_MODEL), layer),
            _const_spec((POOL_WIDTH, D_MODEL), layer),
            _const_spec((CONV_WIDTH, D_MODEL), layer),
            _const_spec((D_MODEL, D_MODEL), layer),
        ],
        out_specs=[
            full((rows, D_MODEL)),
            full((rows, ATTN_WIDTH)),
            full((rows, ATTN_WIDTH)),
            full((n_streams, POOL_HIST, POOL_WIDTH)),
            full((n_streams, CONV_K - 1, CONV_WIDTH)),
        ],
        out_shape=[
            jax.ShapeDtypeStruct((rows, D_MODEL), F32),
            jax.ShapeDtypeStruct((rows, ATTN_WIDTH), F32),
            jax.ShapeDtypeStruct((rows, ATTN_WIDTH), F32),
            jax.ShapeDtypeStruct((n_streams, POOL_HIST, POOL_WIDTH), F32),
            jax.ShapeDtypeStruct((n_streams, CONV_K - 1, CONV_WIDTH), F32),
        ],
        scratch_shapes=[
            pltpu.VMEM((rows, D_MODEL), BF16),
            pltpu.VMEM((rows, ATTN_WIDTH), BF16),
            pltpu.VMEM((rows, ATTN_WIDTH), BF16),
            pltpu.VMEM((rows, ATTN_WIDTH), BF16),
            pltpu.VMEM((rows, ATTN_WIDTH), F32),
            pltpu.VMEM((rows, POOL_WIDTH), F32),
            pltpu.VMEM((rows, POOL_WIDTH), F32),
            pltpu.VMEM((rows, CONV_WIDTH), F32),
            pltpu.VMEM((rows, CONV_WIDTH), F32),
            pltpu.VMEM((_POOL_PAD + t_new, POOL_WIDTH), F32),
            pltpu.VMEM((_CONV_PAD + t_new, CONV_WIDTH), F32),
        ],
        compiler_params=pltpu.CompilerParams(
            dimension_semantics=("arbitrary",), vmem_limit_bytes=_VMEM_LIMIT_BYTES),
        name="mix_sample",
    )(x, p["mix_norm"], p["mix_w_in"], p["gate_b"], p["q_norm"], p["k_norm"], p["head_ones"],
      p["bias_cache"], p["bias_new"], p["head_mask"], cache_kt, cache_vt, state_pool, state_conv,
      p["pool_w"], p["pool_scale"], p["conv_w"], p["conv_b"],
      p["w_branch_attn"], p["w_branch_pool"], p["w_branch_conv"], p["w_out"])


def _rel_lookup(rel_bias, lo, hi):
    lead = rel_bias.shape[:-1]
    n_left = max(0, min(hi, -REL_CLIP) - lo)
    n_right = max(0, hi - max(lo, REL_CLIP + 1))
    a, b = max(lo, -REL_CLIP), min(hi, REL_CLIP + 1)
    parts = []
    if n_left:
        parts.append(jnp.broadcast_to(rel_bias[..., :1], lead + (n_left,)))
    if b > a:
        parts.append(rel_bias[..., a + REL_CLIP:b + REL_CLIP])
    if n_right:
        parts.append(jnp.broadcast_to(rel_bias[..., -1:], lead + (n_right,)))
    return jnp.concatenate(parts, axis=-1)


def _toeplitz(g, rows):
    n = g.shape[-1]
    lead = g.shape[:-1]
    t = jnp.broadcast_to(g[..., None, :], lead + (rows, n)).reshape(lead + (rows * n,))
    return t[..., :rows * (n - 1)].reshape(lead + (rows, n - 1))


def _band_bias(rel_bias, n_query, n_past, n_key):
    g = jnp.concatenate([_rel_lookup(rel_bias, -n_past, n_key - n_past),
                         _rel_lookup(rel_bias, -n_past - n_query, -n_past)], axis=-1)
    return _toeplitz(g, n_query)[..., :n_key].astype(F32)


def _prompt_bias(rel_bias):
    band = _band_bias(rel_bias, CHUNK, N_PREV_CHUNKS * CHUNK, BAND) * LOG2E
    masked = jnp.full(band.shape[:-1] + (KB - BAND,), NEG_INF, F32)
    return jnp.concatenate([jnp.concatenate([band, masked], axis=-1),
                            jnp.concatenate([masked, band], axis=-1)], axis=-2)


def _sample_bias(rel_bias, t_new, r_cache):
    depth = rel_bias.shape[0]
    table = _band_bias(rel_bias, t_new, r_cache, r_cache + t_new)
    table = table.reshape(depth, N_HEADS * t_new, r_cache + t_new)
    return table[..., :r_cache], table[..., r_cache:]


def kernel(x_prompt, x_sample, cache_attn_k, cache_attn_v, state_pool, state_conv, ffn1_norm, ffn1_w_in, ffn1_w_out, mix_norm, mix_w_in, gate_b, q_norm, k_norm, rel_bias, pool_w, pool_scale, conv_w, conv_b, w_branch_attn, w_branch_pool, w_branch_conv, w_out, ffn2_norm, ffn2_w_in, ffn2_w_out):
    batch, seq, _ = x_prompt.shape
    n_streams, t_new, _ = x_sample.shape
    depth = mix_w_in.shape[0]
    r_cache = cache_attn_k.shape[2]
    assert batch == 1 and seq % TM == 0 and seq >= TM
    assert r_cache == N_PREV_CHUNKS * CHUNK and PAST_LEN % CHUNK == 0 and t_new <= CHUNK

    row = lambda a: a.reshape(depth, 1, a.shape[-1])
    head = np.arange(ATTN_WIDTH) // HEAD_DIM
    n_groups = len(POOL_WINDOWS)
    pool_bd = jnp.einsum("lgcd,gh->lgchd", pool_w, jnp.eye(n_groups, dtype=pool_w.dtype))
    bias_cache, bias_new = _sample_bias(rel_bias, t_new, r_cache)
    p = dict(
        mix_norm=row(mix_norm), mix_w_in=mix_w_in.astype(BF16), gate_b=row(gate_b),
        q_norm=row(jnp.tile(q_norm, (1, N_HEADS))), k_norm=row(jnp.tile(k_norm, (1, N_HEADS))),
        head_ones=jnp.asarray(head[:, None] == head[None, :], BF16),
        bias_prompt=_prompt_bias(rel_bias), bias_cache=bias_cache, bias_new=bias_new,
        head_mask=jnp.asarray(np.repeat(np.arange(N_HEADS), t_new)[:, None] == head[None, :], F32),
        pool_w=pool_bd.reshape(depth, POOL_WIDTH, POOL_WIDTH).astype(BF16),
        pool_scale=row(pool_scale), conv_w=conv_w, conv_b=row(conv_b),
        w_branch_attn=w_branch_attn.astype(BF16), w_branch_pool=w_branch_pool.astype(BF16),
        w_branch_conv=w_branch_conv.astype(BF16), w_out=w_out.astype(BF16),
    )
    f1 = (row(ffn1_norm), ffn1_w_in.astype(BF16), ffn1_w_out.astype(BF16))
    f2 = (row(ffn2_norm), ffn2_w_in.astype(BF16), ffn2_w_out.astype(BF16))

    cache_kt = cache_attn_k.transpose(0, 1, 3, 4, 2).reshape(depth, n_streams, ATTN_WIDTH, r_cache)
    cache_vt = cache_attn_v.transpose(0, 1, 3, 4, 2).reshape(depth, n_streams, ATTN_WIDTH, r_cache)

    xp = x_prompt.reshape(seq, D_MODEL)
    xs = x_sample.reshape(n_streams * t_new, D_MODEL)
    outs = [[] for _ in range(8)]
    for l in range(depth):
        xp = _ffn(xp, *f1, l)
        xs = _ffn(xs, *f1, l)
        xp, kp, vp, pp, cp = _mix_prompt(xp, l, p)
        xs, ks, vs, ps, cs = _mix_sample(xs, cache_kt, cache_vt, state_pool, state_conv, l, p)
        xp = _ffn(xp, *f2, l)
        xs = _ffn(xs, *f2, l)
        for acc, val in zip(outs, (kp, vp, pp, cp, ks, vs, ps, cs)):
            acc.append(val)
    kp, vp, pp, cp, ks, vs, ps, cs = (jnp.stack(o) for o in outs)
    r_keep = min(N_PREV_CHUNKS * CHUNK, seq)
    return (xp.reshape(batch, seq, D_MODEL), xs.reshape(n_streams, t_new, D_MODEL),
            kp.reshape(depth, batch, r_keep, N_HEADS, HEAD_DIM),
            vp.reshape(depth, batch, r_keep, N_HEADS, HEAD_DIM),
            pp.reshape(depth, batch, POOL_HIST, POOL_WIDTH),
            cp.reshape(depth, batch, CONV_K - 1, CONV_WIDTH),
            ks.reshape(depth, n_streams, t_new, N_HEADS, HEAD_DIM),
            vs.reshape(depth, n_streams, t_new, N_HEADS, HEAD_DIM),
            ps, cs)
```

```python
import functools

import jax
import jax.numpy as jnp
import numpy as np
from jax import lax
from jax.experimental import pallas as pl
from jax.experimental.pallas import tpu as pltpu

D_MODEL = 1024
DEPTH = 4
PAST_LEN = 1024
CHUNK = 64
N_PREV_CHUNKS = 8
BAND = (N_PREV_CHUNKS + 1) * CHUNK
N_HEADS = 8
HEAD_DIM = 64
ATTN_WIDTH = N_HEADS * HEAD_DIM
ATTN_SCALE = HEAD_DIM ** -0.5
REL_CLIP = 128
POOL_WINDOWS = (2, 4, 8, 16)
POOL_GROUP = 64
POOL_WIDTH = len(POOL_WINDOWS) * POOL_GROUP
POOL_HIST = max(POOL_WINDOWS) - 1
CONV_WIDTH = 256
CONV_K = 3
D_FF = 2816
EPS = 1e-6
NEG_INF = -1e30

_Q0, _K0, _V0 = 0, ATTN_WIDTH, 2 * ATTN_WIDTH
_P0 = 3 * ATTN_WIDTH
_CU0 = _P0 + POOL_WIDTH
_CB0 = _CU0 + CONV_WIDTH
_CC0 = _CB0 + CONV_WIDTH
_G0 = _CC0 + CONV_WIDTH

TM = N_PREV_CHUNKS * CHUNK
QB = 2 * CHUNK
KB = BAND + CHUNK
QH = 2 * QB
KH = KB + QB
VW = 2 * HEAD_DIM
LOG2E = 1.4426950408889634
_FF_CHUNKS = ((0, 1024), (1024, 2048), (2048, D_FF))
_FFN_TM = 2 * TM
_POOL_PAD = 16
_CONV_PAD = 8

_V7X_VMEM_BYTES = 64 * 2 ** 20
_VMEM_LIMIT_BYTES = _V7X_VMEM_BYTES - 8 * 2 ** 20

F32 = jnp.float32
BF16 = jnp.bfloat16


def _dot(a, b):
    return jnp.dot(a, b, preferred_element_type=F32)


def _dot_nt(a, b):
    return lax.dot_general(a, b, (((1,), (1,)), ((), ())), preferred_element_type=F32)


def _rms(x, g):
    ms = jnp.mean(x * x, axis=-1, keepdims=True)
    return x * lax.rsqrt(ms + EPS) * g


def _head_rms2(q, k, qg, kg, head_ones_ref):
    rows = q.shape[0]
    sq = jnp.concatenate([(q * q).astype(BF16), (k * k).astype(BF16)], axis=0)
    ss = _dot(sq, head_ones_ref[...]) * (1.0 / HEAD_DIM)
    return (q * lax.rsqrt(ss[:rows] + EPS) * qg, k * lax.rsqrt(ss[rows:] + EPS) * kg)


def _pool_delta(ext_ref, pu, rows, pos0):
    def back(j):
        return ext_ref[_POOL_PAD - j:_POOL_PAD - j + rows, :]
    s2 = pu + back(1)
    s4 = s2 + back(2) + back(3)
    s8 = s4 + back(4) + back(5) + back(6) + back(7)
    s16 = s8
    for j in range(8, 16):
        s16 = s16 + back(j)
    lane = lax.broadcasted_iota(jnp.int32, (rows, POOL_WIDTH), 1)
    pos = lax.broadcasted_iota(jnp.int32, (rows, POOL_WIDTH), 0) + pos0
    g0, g1, g2 = lane < POOL_GROUP, lane < 2 * POOL_GROUP, lane < 3 * POOL_GROUP
    wsum = jnp.where(g0, s2, jnp.where(g1, s4, jnp.where(g2, s8, s16)))
    wlen = jnp.where(g0, 2, jnp.where(g1, 4, jnp.where(g2, 8, 16)))
    cnt = jnp.minimum(wlen, pos + 1).astype(F32)
    return wsum / cnt - pu


def _conv_taps(ext_ref, rows, cw_ref, cbias_ref):
    y = cbias_ref[...]
    for t in range(CONV_K):
        off = _CONV_PAD - (CONV_K - 1) + t
        y = y + ext_ref[off:off + rows, :] * cw_ref[t:t + 1, :]
    return y


def _gate(h, idx, win_ref, gb_ref):
    lo, hi = _G0 + idx * D_MODEL, _G0 + (idx + 1) * D_MODEL
    return jax.nn.sigmoid(_dot(h, win_ref[:, lo:hi]) + gb_ref[:, idx * D_MODEL:(idx + 1) * D_MODEL])


def _side_branches(h, o_pool, o_conv, win_ref, gb_ref, wp_ref, wc_ref):
    return (_gate(h, 1, win_ref, gb_ref) * _dot(o_pool.astype(BF16), wp_ref[...])
            + _gate(h, 2, win_ref, gb_ref) * _dot(o_conv.astype(BF16), wc_ref[...]))


def _project_out(x, attn_gate, o_attn, side, wa_ref, wo_ref):
    m = attn_gate * _dot(o_attn.astype(BF16), wa_ref[...]) + side
    return x + _dot(m.astype(BF16), wo_ref[...])


def _ffn_kernel(x_ref, g_ref, win_ref, wout_ref, o_ref):
    x = x_ref[...]
    h = _rms(x, g_ref[...]).astype(BF16)
    acc = None
    for a, b in _FF_CHUNKS:
        gate = _dot(h, win_ref[:, a:b])
        up = _dot(h, win_ref[:, D_FF + a:D_FF + b])
        act = (gate * jax.nn.sigmoid(gate) * up).astype(BF16)
        part = _dot(act, wout_ref[a:b, :])
        acc = part if acc is None else acc + part
    o_ref[...] = x + 0.5 * acc


def _const_spec(shape, layer):
    nd = len(shape)
    return pl.BlockSpec((None,) + tuple(shape), lambda i: (layer,) + (0,) * nd,
                        pipeline_mode=pl.Buffered(1))


def _shared_spec(shape):
    nd = len(shape)
    return pl.BlockSpec(tuple(shape), lambda i: (0,) * nd, pipeline_mode=pl.Buffered(1))


def _ffn(x, norm, w_in, w_out, layer):
    rows = x.shape[0]
    tm = min(_FFN_TM, rows)
    return pl.pallas_call(
        _ffn_kernel,
        grid=(rows // tm,),
        in_specs=[
            pl.BlockSpec((tm, D_MODEL), lambda i: (i, 0)),
            _const_spec((1, D_MODEL), layer),
            _const_spec((D_MODEL, 2 * D_FF), layer),
            _const_spec((D_FF, D_MODEL), layer),
        ],
        out_specs=pl.BlockSpec((tm, D_MODEL), lambda i: (i, 0)),
        out_shape=jax.ShapeDtypeStruct((rows, D_MODEL), F32),
        compiler_params=pltpu.CompilerParams(
            dimension_semantics=("arbitrary",), vmem_limit_bytes=_VMEM_LIMIT_BYTES),
        name="ffn",
    )(x, norm, w_in, w_out)


def _band_attention(i, qbuf, kbuf, vbuf, bias_ref, obuf, s_bufs, p_bufs):
    n_items = (TM // QH) * N_HEADS
    first_valid = jnp.where(i > 0, 0, TM)
    col = lax.broadcasted_iota(jnp.int32, (QB, KB), 1)

    def item(n):
        return n % N_HEADS, (n // N_HEADS) * QH

    def scores(n, s_ref):
        hh, r0 = item(n)
        s_ref[...] = _dot_nt(qbuf[hh, pl.ds(r0, QH), :], kbuf[hh, pl.ds(r0, KH), :])

    def softmax(n, s_ref, p_ref):
        hh, r0 = item(n)
        for a in range(QH // QB):
            sa = s_ref[a * QB:(a + 1) * QB, a * QB:a * QB + KB] + bias_ref[hh]
            sa = jnp.where(col >= first_valid - r0 - a * QB, sa, NEG_INF)
            p_ref[a * QB:(a + 1) * QB, a * QB:a * QB + KB] = jnp.exp2(
                sa - jnp.max(sa, axis=-1, keepdims=True)).astype(BF16)

    def values(n, p_ref):
        hh, r0 = item(n)
        o = _dot(p_ref[...], vbuf[hh, pl.ds(r0, KH), :])
        obuf[hh, pl.ds(r0, QH), :] = o[:, :HEAD_DIM] / o[:, HEAD_DIM:HEAD_DIM + 1]

    s0, s1 = s_bufs
    p0, p1 = p_bufs
    scores(0, s0)
    softmax(0, s0, p0)
    scores(1, s1)

    def group(k, s_cur, p_cur, s_nxt, p_prev):
        values(k - 1, p_prev)
        softmax(k, s_cur, p_cur)
        scores(k + 1, s_nxt)

    for k in range(1, n_items - 1):
        if k % 2:
            group(k, s1, p1, s0, p0)
        else:
            group(k, s0, p0, s1, p1)
    values(n_items - 2, p0)
    softmax(n_items - 1, s1, p1)
    values(n_items - 1, p1)


def _mix_prompt_kernel(x_ref, g_ref, win_ref, gb_ref, qg_ref, kg_ref, ones_ref, bias_ref,
                       pw_ref, ps_ref, cw_ref, cbias_ref, wa_ref, wp_ref, wc_ref, wo_ref,
                       y_ref, ko_ref, vo_ref, po_ref, co_ref,
                       qbuf, kbuf, vbuf, obuf, s0buf, s1buf, p0buf, p1buf, pext, cext):
    i = pl.program_id(0)

    @pl.when(i == 0)
    def _():
        kbuf[:, 0:TM, :] = jnp.zeros((N_HEADS, TM, HEAD_DIM), BF16)
        ones_col = lax.broadcasted_iota(jnp.int32, (N_HEADS, 2 * TM, VW), 2) == HEAD_DIM
        vbuf[...] = jnp.where(ones_col, 1.0, 0.0).astype(BF16)
        pext[0:_POOL_PAD, :] = jnp.zeros((_POOL_PAD, POOL_WIDTH), F32)
        cext[0:_CONV_PAD, :] = jnp.zeros((_CONV_PAD, CONV_WIDTH), F32)
        p0buf[...] = jnp.zeros((QH, KH), BF16)
        p1buf[...] = jnp.zeros((QH, KH), BF16)

    x = x_ref[...]
    h = _rms(x, g_ref[...]).astype(BF16)

    qkv = _dot(h, win_ref[:, _Q0:_P0])
    q = qkv[:, _Q0:_K0]
    k = qkv[:, _K0:_V0]
    v = qkv[:, _V0:_P0]
    qn, kn = _head_rms2(q, k, qg_ref[...], kg_ref[...], ones_ref)
    qn = qn * (ATTN_SCALE * LOG2E)
    ko_ref[...] = kn
    vo_ref[...] = v
    for hh in range(N_HEADS):
        sl = slice(hh * HEAD_DIM, (hh + 1) * HEAD_DIM)
        qbuf[hh] = qn[:, sl].astype(BF16)
        kbuf[hh, TM:2 * TM, :] = kn[:, sl].astype(BF16)
        vbuf[hh, TM:2 * TM, 0:HEAD_DIM] = v[:, sl].astype(BF16)

    side_in = _dot(h, win_ref[:, _P0:_G0])
    pu = side_in[:, 0:POOL_WIDTH]
    pext[_POOL_PAD:_POOL_PAD + TM, :] = pu
    d = _pool_delta(pext, pu, TM, i * TM)
    o_pool = _dot(d.astype(BF16), pw_ref[...]) * ps_ref[...]
    po_ref[...] = pext[_POOL_PAD + TM - POOL_HIST:_POOL_PAD + TM, :]
    pext[0:_POOL_PAD, :] = pext[TM:TM + _POOL_PAD, :]

    cu = side_in[:, _CU0 - _P0:_CB0 - _P0]
    cgate = side_in[:, _CB0 - _P0:_CC0 - _P0]
    cc = side_in[:, _CC0 - _P0:_G0 - _P0]
    cext[_CONV_PAD:_CONV_PAD + TM, :] = cc * cu
    o_conv = cgate * _conv_taps(cext, TM, cw_ref, cbias_ref)
    co_ref[...] = cext[_CONV_PAD + TM - (CONV_K - 1):_CONV_PAD + TM, :]
    cext[0:_CONV_PAD, :] = cext[TM:TM + _CONV_PAD, :]

    side = _side_branches(h, o_pool, o_conv, win_ref, gb_ref, wp_ref, wc_ref)
    attn_gate = _gate(h, 0, win_ref, gb_ref)

    _band_attention(i, qbuf, kbuf, vbuf, bias_ref, obuf, (s0buf, s1buf), (p0buf, p1buf))

    o_attn = jnp.concatenate([obuf[hh] for hh in range(N_HEADS)], axis=1)
    y_ref[...] = _project_out(x, attn_gate, o_attn, side, wa_ref, wo_ref)

    kbuf[:, 0:TM, :] = kbuf[:, TM:2 * TM, :]
    vbuf[:, 0:TM, :] = vbuf[:, TM:2 * TM, :]


def _mix_prompt(x, layer, p):
    rows = x.shape[0]
    in_w = p["mix_w_in"].shape[-1]
    return pl.pallas_call(
        _mix_prompt_kernel,
        grid=(rows // TM,),
        in_specs=[
            pl.BlockSpec((TM, D_MODEL), lambda i: (i, 0)),
            _const_spec((1, D_MODEL), layer),
            _const_spec((D_MODEL, in_w), layer),
            _const_spec((1, 3 * D_MODEL), layer),
            _const_spec((1, ATTN_WIDTH), layer),
            _const_spec((1, ATTN_WIDTH), layer),
            _shared_spec((ATTN_WIDTH, ATTN_WIDTH)),
            _const_spec((N_HEADS, QB, KB), layer),
            _const_spec((POOL_WIDTH, POOL_WIDTH), layer),
            _const_spec((1, POOL_WIDTH), layer),
            _const_spec((CONV_K, CONV_WIDTH), layer),
            _const_spec((1, CONV_WIDTH), layer),
            _const_spec((ATTN_WIDTH, D_MODEL), layer),
            _const_spec((POOL_WIDTH, D_MODEL), layer),
            _const_spec((CONV_WIDTH, D_MODEL), layer),
            _const_spec((D_MODEL, D_MODEL), layer),
        ],
        out_specs=[
            pl.BlockSpec((TM, D_MODEL), lambda i: (i, 0)),
            pl.BlockSpec((TM, ATTN_WIDTH), lambda i: (0, 0)),
            pl.BlockSpec((TM, ATTN_WIDTH), lambda i: (0, 0)),
            pl.BlockSpec((POOL_HIST, POOL_WIDTH), lambda i: (0, 0)),
            pl.BlockSpec((CONV_K - 1, CONV_WIDTH), lambda i: (0, 0)),
        ],
        out_shape=[
            jax.ShapeDtypeStruct((rows, D_MODEL), F32),
            jax.ShapeDtypeStruct((TM, ATTN_WIDTH), F32),
            jax.ShapeDtypeStruct((TM, ATTN_WIDTH), F32),
            jax.ShapeDtypeStruct((POOL_HIST, POOL_WIDTH), F32),
            jax.ShapeDtypeStruct((CONV_K - 1, CONV_WIDTH), F32),
        ],
        scratch_shapes=[
            pltpu.VMEM((N_HEADS, TM, HEAD_DIM), BF16),
            pltpu.VMEM((N_HEADS, 2 * TM, HEAD_DIM), BF16),
            pltpu.VMEM((N_HEADS, 2 * TM, VW), BF16),
            pltpu.VMEM((N_HEADS, TM, HEAD_DIM), F32),
            pltpu.VMEM((QH, KH), F32),
            pltpu.VMEM((QH, KH), F32),
            pltpu.VMEM((QH, KH), BF16),
            pltpu.VMEM((QH, KH), BF16),
            pltpu.VMEM((_POOL_PAD + TM, POOL_WIDTH), F32),
            pltpu.VMEM((_CONV_PAD + TM, CONV_WIDTH), F32),
        ],
        compiler_params=pltpu.CompilerParams(
            dimension_semantics=("arbitrary",), vmem_limit_bytes=_VMEM_LIMIT_BYTES),
        name="mix_prompt",
    )(x, p["mix_norm"], p["mix_w_in"], p["gate_b"], p["q_norm"], p["k_norm"], p["head_ones"],
      p["bias_prompt"], p["pool_w"], p["pool_scale"], p["conv_w"], p["conv_b"],
      p["w_branch_attn"], p["w_branch_pool"], p["w_branch_conv"], p["w_out"])


def _mix_sample_kernel(t_new, x_ref, g_ref, win_ref, gb_ref, qg_ref, kg_ref, ones_ref,
                       b1_ref, b2_ref, hm_ref, kct_ref, vct_ref, sp_ref, sc_ref,
                       pw_ref, ps_ref, cw_ref, cbias_ref, wa_ref, wp_ref, wc_ref, wo_ref,
                       y_ref, ko_ref, vo_ref, po_ref, co_ref,
                       hbuf, qbuf, knbuf, vnbuf, obuf, pubuf, dbuf, cinbuf, ybuf, pext, cext):
    b = pl.program_id(0)
    T = t_new

    @pl.when(b == 0)
    def _():
        h = _rms(x_ref[...], g_ref[...]).astype(BF16)
        hbuf[...] = h
        q = _dot(h, win_ref[:, _Q0:_Q0 + ATTN_WIDTH])
        k = _dot(h, win_ref[:, _K0:_K0 + ATTN_WIDTH])
        v = _dot(h, win_ref[:, _V0:_V0 + ATTN_WIDTH])
        qn, kn = _head_rms2(q, k, qg_ref[...], kg_ref[...], ones_ref)
        qn = qn * ATTN_SCALE
        ko_ref[...] = kn
        vo_ref[...] = v
        qbuf[...] = qn.astype(BF16)
        knbuf[...] = kn.astype(BF16)
        vnbuf[...] = v.astype(BF16)
        pubuf[...] = _dot(h, win_ref[:, _P0:_P0 + POOL_WIDTH])
        cu = _dot(h, win_ref[:, _CU0:_CU0 + CONV_WIDTH])
        cc = _dot(h, win_ref[:, _CC0:_CC0 + CONV_WIDTH])
        cinbuf[...] = cc * cu

    r0 = pl.multiple_of(b * T, T)

    q_b = qbuf[pl.ds(r0, T), :]
    hm = hm_ref[...]
    qexp = jnp.where(hm > 0, jnp.concatenate([q_b] * N_HEADS, axis=0), jnp.zeros((), BF16))
    kct = kct_ref[...].astype(BF16)
    vct = vct_ref[...].astype(BF16)
    s1 = _dot(qexp, kct) + b1_ref[...]
    s2 = _dot_nt(qexp, knbuf[pl.ds(r0, T), :]) + b2_ref[...]
    m = jnp.maximum(jnp.max(s1, axis=-1, keepdims=True), jnp.max(s2, axis=-1, keepdims=True))
    p1 = jnp.exp(s1 - m)
    p2 = jnp.exp(s2 - m)
    denom = jnp.sum(p1, axis=-1, keepdims=True) + jnp.sum(p2, axis=-1, keepdims=True)
    oall = (_dot_nt(p1.astype(BF16), vct) + _dot(p2.astype(BF16), vnbuf[pl.ds(r0, T), :])) / denom
    oall = oall * hm
    o = oall[0:T, :]
    for hh in range(1, N_HEADS):
        o = o + oall[hh * T:(hh + 1) * T, :]
    obuf[pl.ds(r0, T), :] = o

    pu = pubuf[pl.ds(r0, T), :]
    pext[_POOL_PAD - POOL_HIST:_POOL_PAD, :] = sp_ref[b]
    pext[_POOL_PAD:_POOL_PAD + T, :] = pu
    dbuf[pl.ds(r0, T), :] = _pool_delta(pext, pu, T, PAST_LEN)
    po_ref[b] = pext[_POOL_PAD + T - POOL_HIST:_POOL_PAD + T, :]

    cext[_CONV_PAD - (CONV_K - 1):_CONV_PAD, :] = sc_ref[b]
    cext[_CONV_PAD:_CONV_PAD + T, :] = cinbuf[pl.ds(r0, T), :]
    ybuf[pl.ds(r0, T), :] = _conv_taps(cext, T, cw_ref, cbias_ref)
    co_ref[b] = cext[_CONV_PAD + T - (CONV_K - 1):_CONV_PAD + T, :]

    @pl.when(b == pl.num_programs(0) - 1)
    def _():
        h = hbuf[...]
        o_pool = _dot(dbuf[...].astype(BF16), pw_ref[...]) * ps_ref[...]
        o_conv = _dot(h, win_ref[:, _CB0:_CB0 + CONV_WIDTH]) * ybuf[...]
        side = _side_branches(h, o_pool, o_conv, win_ref, gb_ref, wp_ref, wc_ref)
        y_ref[...] = _project_out(x_ref[...], _gate(h, 0, win_ref, gb_ref), obuf[...], side,
                                  wa_ref, wo_ref)


def _mix_sample(x, cache_kt, cache_vt, state_pool, state_conv, layer, p):
    rows = x.shape[0]
    n_streams, r_cache = cache_kt.shape[1], cache_kt.shape[3]
    t_new = rows // n_streams
    in_w = p["mix_w_in"].shape[-1]
    cache_spec = pl.BlockSpec((None, None, ATTN_WIDTH, r_cache), lambda b: (layer, b, 0, 0))
    full = lambda shape: pl.BlockSpec(tuple(shape), lambda b: (0,) * len(shape))
    return pl.pallas_call(
        functools.partial(_mix_sample_kernel, t_new),
        grid=(n_streams,),
        in_specs=[
            _shared_spec((rows, D_MODEL)),
            _const_spec((1, D_MODEL), layer),
            _const_spec((D_MODEL, in_w), layer),
            _const_spec((1, 3 * D_MODEL), layer),
            _const_spec((1, ATTN_WIDTH), layer),
            _const_spec((1, ATTN_WIDTH), layer),
            _shared_spec((ATTN_WIDTH, ATTN_WIDTH)),
            _const_spec((N_HEADS * t_new, r_cache), layer),
            _const_spec((N_HEADS * t_new, t_new), layer),
            _shared_spec((N_HEADS * t_new, ATTN_WIDTH)),
            cache_spec,
            cache_spec,
            _const_spec((n_streams, POOL_HIST, POOL_WIDTH), layer),
            _const_spec((n_streams, CONV_K - 1, CONV_WIDTH), layer),
            _const_spec((POOL_WIDTH, POOL_WIDTH), layer),
            _const_spec((1, POOL_WIDTH), layer),
            _const_spec((CONV_K, CONV_WIDTH), layer),
            _const_spec((1, CONV_WIDTH), layer),
            _const_spec((ATTN_WIDTH, D_MODEL), layer),
            _const_spec((POOL_WIDTH, D_MODEL), layer),
            _const_spec((CONV_WIDTH, D_MODEL), layer),
            _const_spec((D_MODEL, D_MODEL), layer),
        ],
        out_specs=[
            full((rows, D_MODEL)),
            full((rows, ATTN_WIDTH)),
            full((rows, ATTN_WIDTH)),
            full((n_streams, POOL_HIST, POOL_WIDTH)),
            full((n_streams, CONV_K - 1, CONV_WIDTH)),
        ],
        out_shape=[
            jax.ShapeDtypeStruct((rows, D_MODEL), F32),
            jax.ShapeDtypeStruct((rows, ATTN_WIDTH), F32),
            jax.ShapeDtypeStruct((rows, ATTN_WIDTH), F32),
            jax.ShapeDtypeStruct((n_streams, POOL_HIST, POOL_WIDTH), F32),
            jax.ShapeDtypeStruct((n_streams, CONV_K - 1, CONV_WIDTH), F32),
        ],
        scratch_shapes=[
            pltpu.VMEM((rows, D_MODEL), BF16),
            pltpu.VMEM((rows, ATTN_WIDTH), BF16),
            pltpu.VMEM((rows, ATTN_WIDTH), BF16),
            pltpu.VMEM((rows, ATTN_WIDTH), BF16),
            pltpu.VMEM((rows, ATTN_WIDTH), F32),
            pltpu.VMEM((rows, POOL_WIDTH), F32),
            pltpu.VMEM((rows, POOL_WIDTH), F32),
            pltpu.VMEM((rows, CONV_WIDTH), F32),
            pltpu.VMEM((rows, CONV_WIDTH), F32),
            pltpu.VMEM((_POOL_PAD + t_new, POOL_WIDTH), F32),
            pltpu.VMEM((_CONV_PAD + t_new, CONV_WIDTH), F32),
        ],
        compiler_params=pltpu.CompilerParams(
            dimension_semantics=("arbitrary",), vmem_limit_bytes=_VMEM_LIMIT_BYTES),
        name="mix_sample",
    )(x, p["mix_norm"], p["mix_w_in"], p["gate_b"], p["q_norm"], p["k_norm"], p["head_ones"],
      p["bias_cache"], p["bias_new"], p["head_mask"], cache_kt, cache_vt, state_pool, state_conv,
      p["pool_w"], p["pool_scale"], p["conv_w"], p["conv_b"],
      p["w_branch_attn"], p["w_branch_pool"], p["w_branch_conv"], p["w_out"])


def _rel_lookup(rel_bias, lo, hi):
    lead = rel_bias.shape[:-1]
    n_left = max(0, min(hi, -REL_CLIP) - lo)
    n_right = max(0, hi - max(lo, REL_CLIP + 1))
    a, b = max(lo, -REL_CLIP), min(hi, REL_CLIP + 1)
    parts = []
    if n_left:
        parts.append(jnp.broadcast_to(rel_bias[..., :1], lead + (n_left,)))
    if b > a:
        parts.append(rel_bias[..., a + REL_CLIP:b + REL_CLIP])
    if n_right:
        parts.append(jnp.broadcast_to(rel_bias[..., -1:], lead + (n_right,)))
    return jnp.concatenate(parts, axis=-1)


def _toeplitz(g, rows):
    n = g.shape[-1]
    lead = g.shape[:-1]
    t = jnp.broadcast_to(g[..., None, :], lead + (rows, n)).reshape(lead + (rows * n,))
    return t[..., :rows * (n - 1)].reshape(lead + (rows, n - 1))


def _band_bias(rel_bias, n_query, n_past, n_key):
    g = jnp.concatenate([_rel_lookup(rel_bias, -n_past, n_key - n_past),
                         _rel_lookup(rel_bias, -n_past - n_query, -n_past)], axis=-1)
    return _toeplitz(g, n_query)[..., :n_key].astype(F32)


def _prompt_bias(rel_bias):
    band = _band_bias(rel_bias, CHUNK, N_PREV_CHUNKS * CHUNK, BAND) * LOG2E
    masked = jnp.full(band.shape[:-1] + (KB - BAND,), NEG_INF, F32)
    return jnp.concatenate([jnp.concatenate([band, masked], axis=-1),
                            jnp.concatenate([masked, band], axis=-1)], axis=-2)


def _sample_bias(rel_bias, t_new, r_cache):
    depth = rel_bias.shape[0]
    table = _band_bias(rel_bias, t_new, r_cache, r_cache + t_new)
    table = table.reshape(depth, N_HEADS * t_new, r_cache + t_new)
    return table[..., :r_cache], table[..., r_cache:]


def kernel(x_prompt, x_sample, cache_attn_k, cache_attn_v, state_pool, state_conv, ffn1_norm, ffn1_w_in, ffn1_w_out, mix_norm, mix_w_in, gate_b, q_norm, k_norm, rel_bias, pool_w, pool_scale, conv_w, conv_b, w_branch_attn, w_branch_pool, w_branch_conv, w_out, ffn2_norm, ffn2_w_in, ffn2_w_out):
    batch, seq, _ = x_prompt.shape
    n_streams, t_new, _ = x_sample.shape
    depth = mix_w_in.shape[0]
    r_cache = cache_attn_k.shape[2]
    assert batch == 1 and seq % TM == 0 and seq >= TM
    assert r_cache == N_PREV_CHUNKS * CHUNK and PAST_LEN % CHUNK == 0 and t_new <= CHUNK

    row = lambda a: a.reshape(depth, 1, a.shape[-1])
    head = np.arange(ATTN_WIDTH) // HEAD_DIM
    n_groups = len(POOL_WINDOWS)
    pool_bd = jnp.einsum("lgcd,gh->lgchd", pool_w, jnp.eye(n_groups, dtype=pool_w.dtype))
    bias_cache, bias_new = _sample_bias(rel_bias, t_new, r_cache)
    p = dict(
        mix_norm=row(mix_norm), mix_w_in=mix_w_in.astype(BF16), gate_b=row(gate_b),
        q_norm=row(jnp.tile(q_norm, (1, N_HEADS))), k_norm=row(jnp.tile(k_norm, (1, N_HEADS))),
        head_ones=jnp.asarray(head[:, None] == head[None, :], BF16),
        bias_prompt=_prompt_bias(rel_bias), bias_cache=bias_cache, bias_new=bias_new,
        head_mask=jnp.asarray(np.repeat(np.arange(N_HEADS), t_new)[:, None] == head[None, :], F32),
        pool_w=pool_bd.reshape(depth, POOL_WIDTH, POOL_WIDTH).astype(BF16),
        pool_scale=row(pool_scale), conv_w=conv_w, conv_b=row(conv_b),
        w_branch_attn=w_branch_attn.astype(BF16), w_branch_pool=w_branch_pool.astype(BF16),
        w_branch_conv=w_branch_conv.astype(BF16), w_out=w_out.astype(BF16),
    )
    f1 = (row(ffn1_norm), ffn1_w_in.astype(BF16), ffn1_w_out.astype(BF16))
    f2 = (row(ffn2_norm), ffn2_w_in.astype(BF16), ffn2_w_out.astype(BF16))

    cache_kt = cache_attn_k.transpose(0, 1, 3, 4, 2).reshape(depth, n_streams, ATTN_WIDTH, r_cache)
    cache_vt = cache_attn_v.transpose(0, 1, 3, 4, 2).reshape(depth, n_streams, ATTN_WIDTH, r_cache)

    xp = x_prompt.reshape(seq, D_MODEL)
    xs = x_sample.reshape(n_streams * t_new, D_MODEL)
    outs = [[] for _ in range(8)]
    for l in range(depth):
        xp = _ffn(xp, *f1, l)
        xs = _ffn(xs, *f1, l)
        xp, kp, vp, pp, cp = _mix_prompt(xp, l, p)
        xs, ks, vs, ps, cs = _mix_sample(xs, cache_kt, cache_vt, state_pool, state_conv, l, p)
        xp = _ffn(xp, *f2, l)
        xs = _ffn(xs, *f2, l)
        for acc, val in zip(outs, (kp, vp, pp, cp, ks, vs, ps, cs)):
            acc.append(val)
    kp, vp, pp, cp, ks, vs, ps, cs = (jnp.stack(o) for o in outs)
    r_keep = min(N_PREV_CHUNKS * CHUNK, seq)
    return (xp.reshape(batch, seq, D_MODEL), xs.reshape(n_streams, t_new, D_MODEL),
            kp.reshape(depth, batch, r_keep, N_HEADS, HEAD_DIM),
            vp.reshape(depth, batch, r_keep, N_HEADS, HEAD_DIM),
            pp.reshape(depth, batch, POOL_HIST, POOL_WIDTH),
            cp.reshape(depth, batch, CONV_K - 1, CONV_WIDTH),
            ks.reshape(depth, n_streams, t_new, N_HEADS, HEAD_DIM),
            vs.reshape(depth, n_streams, t_new, N_HEADS, HEAD_DIM),
            ps, cs)
```

```python
import functools

import jax
import jax.numpy as jnp
import numpy as np
from jax import lax
from jax.experimental import pallas as pl
from jax.experimental.pallas import tpu as pltpu

D_MODEL = 1024
DEPTH = 4
PAST_LEN = 1024
CHUNK = 64
N_PREV_CHUNKS = 8
BAND = (N_PREV_CHUNKS + 1) * CHUNK
N_HEADS = 8
HEAD_DIM = 64
ATTN_WIDTH = N_HEADS * HEAD_DIM
ATTN_SCALE = HEAD_DIM ** -0.5
REL_CLIP = 128
POOL_WINDOWS = (2, 4, 8, 16)
POOL_GROUP = 64
POOL_WIDTH = len(POOL_WINDOWS) * POOL_GROUP
POOL_HIST = max(POOL_WINDOWS) - 1
CONV_WIDTH = 256
CONV_K = 3
D_FF = 2816
EPS = 1e-6
NEG_INF = -1e30

_Q0, _K0, _V0 = 0, ATTN_WIDTH, 2 * ATTN_WIDTH
_P0 = 3 * ATTN_WIDTH
_CU0 = _P0 + POOL_WIDTH
_CB0 = _CU0 + CONV_WIDTH
_CC0 = _CB0 + CONV_WIDTH
_G0 = _CC0 + CONV_WIDTH

TM = N_PREV_CHUNKS * CHUNK
QB = 2 * CHUNK
KB = BAND + CHUNK
QH = 2 * QB
KH = KB + QB
VW = 2 * HEAD_DIM
LOG2E = 1.4426950408889634
_FF_CHUNKS = ((0, 1024), (1024, 2048), (2048, D_FF))
_FFN_TM = 2 * TM
_POOL_PAD = 16
_CONV_PAD = 8
_SAMPLE_STREAMS_PER_STEP = 4

_V7X_VMEM_BYTES = 64 * 2 ** 20
_VMEM_LIMIT_BYTES = _V7X_VMEM_BYTES - 8 * 2 ** 20

F32 = jnp.float32
BF16 = jnp.bfloat16


def _dot(a, b):
    return jnp.dot(a, b, preferred_element_type=F32)


def _dot_nt(a, b):
    return lax.dot_general(a, b, (((1,), (1,)), ((), ())), preferred_element_type=F32)


def _rms(x, g):
    ms = jnp.mean(x * x, axis=-1, keepdims=True)
    return x * lax.rsqrt(ms + EPS) * g


def _head_rms2(q, k, qg, kg, head_ones_ref):
    rows = q.shape[0]
    sq = jnp.concatenate([(q * q).astype(BF16), (k * k).astype(BF16)], axis=0)
    ss = _dot(sq, head_ones_ref[...]) * (1.0 / HEAD_DIM)
    return (q * lax.rsqrt(ss[:rows] + EPS) * qg, k * lax.rsqrt(ss[rows:] + EPS) * kg)


def _pool_delta(ext_ref, pu, rows, pos0):
    def back(j):
        return ext_ref[_POOL_PAD - j:_POOL_PAD - j + rows, :]
    s2 = pu + back(1)
    s4 = s2 + back(2) + back(3)
    s8 = s4 + back(4) + back(5) + back(6) + back(7)
    s16 = s8
    for j in range(8, 16):
        s16 = s16 + back(j)
    lane = lax.broadcasted_iota(jnp.int32, (rows, POOL_WIDTH), 1)
    pos = lax.broadcasted_iota(jnp.int32, (rows, POOL_WIDTH), 0) + pos0
    g0, g1, g2 = lane < POOL_GROUP, lane < 2 * POOL_GROUP, lane < 3 * POOL_GROUP
    wsum = jnp.where(g0, s2, jnp.where(g1, s4, jnp.where(g2, s8, s16)))
    wlen = jnp.where(g0, 2, jnp.where(g1, 4, jnp.where(g2, 8, 16)))
    cnt = jnp.minimum(wlen, pos + 1).astype(F32)
    return wsum / cnt - pu


def _conv_taps(ext_ref, rows, cw_ref, cbias_ref):
    y = cbias_ref[...]
    for t in range(CONV_K):
        off = _CONV_PAD - (CONV_K - 1) + t
        y = y + ext_ref[off:off + rows, :] * cw_ref[t:t + 1, :]
    return y


def _gate(h, idx, win_ref, gb_ref):
    lo, hi = _G0 + idx * D_MODEL, _G0 + (idx + 1) * D_MODEL
    return jax.nn.sigmoid(_dot(h, win_ref[:, lo:hi]) + gb_ref[:, idx * D_MODEL:(idx + 1) * D_MODEL])


def _side_branches(h, o_pool, o_conv, win_ref, gb_ref, wp_ref, wc_ref):
    return (_gate(h, 1, win_ref, gb_ref) * _dot(o_pool.astype(BF16), wp_ref[...])
            + _gate(h, 2, win_ref, gb_ref) * _dot(o_conv.astype(BF16), wc_ref[...]))


def _project_out(x, attn_gate, o_attn, side, wa_ref, wo_ref):
    m = attn_gate * _dot(o_attn.astype(BF16), wa_ref[...]) + side
    return x + _dot(m.astype(BF16), wo_ref[...])


def _ffn_kernel(x_ref, g_ref, win_ref, wout_ref, o_ref):
    x = x_ref[...]
    h = _rms(x, g_ref[...]).astype(BF16)
    acc = None
    for a, b in _FF_CHUNKS:
        gate = _dot(h, win_ref[:, a:b])
        up = _dot(h, win_ref[:, D_FF + a:D_FF + b])
        act = (gate * jax.nn.sigmoid(gate) * up).astype(BF16)
        part = _dot(act, wout_ref[a:b, :])
        acc = part if acc is None else acc + part
    o_ref[...] = x + 0.5 * acc


def _const_spec(shape, layer):
    nd = len(shape)
    return pl.BlockSpec((None,) + tuple(shape), lambda i: (layer,) + (0,) * nd,
                        pipeline_mode=pl.Buffered(1))


def _shared_spec(shape):
    nd = len(shape)
    return pl.BlockSpec(tuple(shape), lambda i: (0,) * nd, pipeline_mode=pl.Buffered(1))


def _ffn(x, norm, w_in, w_out, layer):
    rows = x.shape[0]
    tm = min(_FFN_TM, rows)
    return pl.pallas_call(
        _ffn_kernel,
        grid=(rows // tm,),
        in_specs=[
            pl.BlockSpec((tm, D_MODEL), lambda i: (i, 0)),
            _const_spec((1, D_MODEL), layer),
            _const_spec((D_MODEL, 2 * D_FF), layer),
            _const_spec((D_FF, D_MODEL), layer),
        ],
        out_specs=pl.BlockSpec((tm, D_MODEL), lambda i: (i, 0)),
        out_shape=jax.ShapeDtypeStruct((rows, D_MODEL), F32),
        compiler_params=pltpu.CompilerParams(
            dimension_semantics=("arbitrary",), vmem_limit_bytes=_VMEM_LIMIT_BYTES),
        name="ffn",
    )(x, norm, w_in, w_out)


def _band_attention(i, qbuf, kbuf, vbuf, bias_ref, obuf, s_bufs, p_bufs):
    n_items = (TM // QH) * N_HEADS
    first_valid = jnp.where(i > 0, 0, TM)
    col = lax.broadcasted_iota(jnp.int32, (QB, KB), 1)

    def item(n):
        return n % N_HEADS, (n // N_HEADS) * QH

    def scores(n, s_ref):
        hh, r0 = item(n)
        s_ref[...] = _dot_nt(qbuf[hh, pl.ds(r0, QH), :], kbuf[hh, pl.ds(r0, KH), :])

    def softmax(n, s_ref, p_ref):
        hh, r0 = item(n)
        for a in range(QH // QB):
            sa = s_ref[a * QB:(a + 1) * QB, a * QB:a * QB + KB] + bias_ref[hh]
            sa = jnp.where(col >= first_valid - r0 - a * QB, sa, NEG_INF)
            p_ref[a * QB:(a + 1) * QB, a * QB:a * QB + KB] = jnp.exp2(
                sa - jnp.max(sa, axis=-1, keepdims=True)).astype(BF16)

    def values(n, p_ref):
        hh, r0 = item(n)
        o = _dot(p_ref[...], vbuf[hh, pl.ds(r0, KH), :])
        obuf[hh, pl.ds(r0, QH), :] = o[:, :HEAD_DIM] / o[:, HEAD_DIM:HEAD_DIM + 1]

    s0, s1 = s_bufs
    p0, p1 = p_bufs
    scores(0, s0)
    softmax(0, s0, p0)
    scores(1, s1)

    def group(k, s_cur, p_cur, s_nxt, p_prev):
        values(k - 1, p_prev)
        softmax(k, s_cur, p_cur)
        scores(k + 1, s_nxt)

    for k in range(1, n_items - 1):
        if k % 2:
            group(k, s1, p1, s0, p0)
        else:
            group(k, s0, p0, s1, p1)
    values(n_items - 2, p0)
    softmax(n_items - 1, s1, p1)
    values(n_items - 1, p1)


def _mix_prompt_kernel(x_ref, g_ref, win_ref, gb_ref, qg_ref, kg_ref, ones_ref, bias_ref,
                       pw_ref, ps_ref, cw_ref, cbias_ref, wa_ref, wp_ref, wc_ref, wo_ref,
                       y_ref, ko_ref, vo_ref, po_ref, co_ref,
                       qbuf, kbuf, vbuf, obuf, s0buf, s1buf, p0buf, p1buf, pext, cext):
    i = pl.program_id(0)

    @pl.when(i == 0)
    def _():
        kbuf[:, 0:TM, :] = jnp.zeros((N_HEADS, TM, HEAD_DIM), BF16)
        ones_col = lax.broadcasted_iota(jnp.int32, (N_HEADS, 2 * TM, VW), 2) == HEAD_DIM
        vbuf[...] = jnp.where(ones_col, 1.0, 0.0).astype(BF16)
        pext[0:_POOL_PAD, :] = jnp.zeros((_POOL_PAD, POOL_WIDTH), F32)
        cext[0:_CONV_PAD, :] = jnp.zeros((_CONV_PAD, CONV_WIDTH), F32)
        p0buf[...] = jnp.zeros((QH, KH), BF16)
        p1buf[...] = jnp.zeros((QH, KH), BF16)

    x = x_ref[...]
    h = _rms(x, g_ref[...]).astype(BF16)

    qkv = _dot(h, win_ref[:, _Q0:_P0])
    q = qkv[:, _Q0:_K0]
    k = qkv[:, _K0:_V0]
    v = qkv[:, _V0:_P0]
    qn, kn = _head_rms2(q, k, qg_ref[...], kg_ref[...], ones_ref)
    qn = qn * (ATTN_SCALE * LOG2E)
    ko_ref[...] = kn
    vo_ref[...] = v
    for hh in range(N_HEADS):
        sl = slice(hh * HEAD_DIM, (hh + 1) * HEAD_DIM)
        qbuf[hh] = qn[:, sl].astype(BF16)
        kbuf[hh, TM:2 * TM, :] = kn[:, sl].astype(BF16)
        vbuf[hh, TM:2 * TM, 0:HEAD_DIM] = v[:, sl].astype(BF16)

    side_in = _dot(h, win_ref[:, _P0:_G0])
    pu = side_in[:, 0:POOL_WIDTH]
    pext[_POOL_PAD:_POOL_PAD + TM, :] = pu
    d = _pool_delta(pext, pu, TM, i * TM)
    o_pool = _dot(d.astype(BF16), pw_ref[...]) * ps_ref[...]
    po_ref[...] = pext[_POOL_PAD + TM - POOL_HIST:_POOL_PAD + TM, :]
    pext[0:_POOL_PAD, :] = pext[TM:TM + _POOL_PAD, :]

    cu = side_in[:, _CU0 - _P0:_CB0 - _P0]
    cgate = side_in[:, _CB0 - _P0:_CC0 - _P0]
    cc = side_in[:, _CC0 - _P0:_G0 - _P0]
    cext[_CONV_PAD:_CONV_PAD + TM, :] = cc * cu
    o_conv = cgate * _conv_taps(cext, TM, cw_ref, cbias_ref)
    co_ref[...] = cext[_CONV_PAD + TM - (CONV_K - 1):_CONV_PAD + TM, :]
    cext[0:_CONV_PAD, :] = cext[TM:TM + _CONV_PAD, :]

    side = _side_branches(h, o_pool, o_conv, win_ref, gb_ref, wp_ref, wc_ref)
    attn_gate = _gate(h, 0, win_ref, gb_ref)

    _band_attention(i, qbuf, kbuf, vbuf, bias_ref, obuf, (s0buf, s1buf), (p0buf, p1buf))

    o_attn = jnp.concatenate([obuf[hh] for hh in range(N_HEADS)], axis=1)
    y_ref[...] = _project_out(x, attn_gate, o_attn, side, wa_ref, wo_ref)

    kbuf[:, 0:TM, :] = kbuf[:, TM:2 * TM, :]
    vbuf[:, 0:TM, :] = vbuf[:, TM:2 * TM, :]


def _mix_prompt(x, layer, p):
    rows = x.shape[0]
    in_w = p["mix_w_in"].shape[-1]
    return pl.pallas_call(
        _mix_prompt_kernel,
        grid=(rows // TM,),
        in_specs=[
            pl.BlockSpec((TM, D_MODEL), lambda i: (i, 0)),
            _const_spec((1, D_MODEL), layer),
            _const_spec((D_MODEL, in_w), layer),
            _const_spec((1, 3 * D_MODEL), layer),
            _const_spec((1, ATTN_WIDTH), layer),
            _const_spec((1, ATTN_WIDTH), layer),
            _shared_spec((ATTN_WIDTH, ATTN_WIDTH)),
            _const_spec((N_HEADS, QB, KB), layer),
            _const_spec((POOL_WIDTH, POOL_WIDTH), layer),
            _const_spec((1, POOL_WIDTH), layer),
            _const_spec((CONV_K, CONV_WIDTH), layer),
            _const_spec((1, CONV_WIDTH), layer),
            _const_spec((ATTN_WIDTH, D_MODEL), layer),
            _const_spec((POOL_WIDTH, D_MODEL), layer),
            _const_spec((CONV_WIDTH, D_MODEL), layer),
            _const_spec((D_MODEL, D_MODEL), layer),
        ],
        out_specs=[
            pl.BlockSpec((TM, D_MODEL), lambda i: (i, 0)),
            pl.BlockSpec((TM, ATTN_WIDTH), lambda i: (0, 0)),
            pl.BlockSpec((TM, ATTN_WIDTH), lambda i: (0, 0)),
            pl.BlockSpec((POOL_HIST, POOL_WIDTH), lambda i: (0, 0)),
            pl.BlockSpec((CONV_K - 1, CONV_WIDTH), lambda i: (0, 0)),
        ],
        out_shape=[
            jax.ShapeDtypeStruct((rows, D_MODEL), F32),
            jax.ShapeDtypeStruct((TM, ATTN_WIDTH), F32),
            jax.ShapeDtypeStruct((TM, ATTN_WIDTH), F32),
            jax.ShapeDtypeStruct((POOL_HIST, POOL_WIDTH), F32),
            jax.ShapeDtypeStruct((CONV_K - 1, CONV_WIDTH), F32),
        ],
        scratch_shapes=[
            pltpu.VMEM((N_HEADS, TM, HEAD_DIM), BF16),
            pltpu.VMEM((N_HEADS, 2 * TM, HEAD_DIM), BF16),
            pltpu.VMEM((N_HEADS, 2 * TM, VW), BF16),
            pltpu.VMEM((N_HEADS, TM, HEAD_DIM), F32),
            pltpu.VMEM((QH, KH), F32),
            pltpu.VMEM((QH, KH), F32),
            pltpu.VMEM((QH, KH), BF16),
            pltpu.VMEM((QH, KH), BF16),
            pltpu.VMEM((_POOL_PAD + TM, POOL_WIDTH), F32),
            pltpu.VMEM((_CONV_PAD + TM, CONV_WIDTH), F32),
        ],
        compiler_params=pltpu.CompilerParams(
            dimension_semantics=("arbitrary",), vmem_limit_bytes=_VMEM_LIMIT_BYTES),
        name="mix_prompt",
    )(x, p["mix_norm"], p["mix_w_in"], p["gate_b"], p["q_norm"], p["k_norm"], p["head_ones"],
      p["bias_prompt"], p["pool_w"], p["pool_scale"], p["conv_w"], p["conv_b"],
      p["w_branch_attn"], p["w_branch_pool"], p["w_branch_conv"], p["w_out"])


def _mix_sample_kernel(t_new, x_ref, g_ref, win_ref, gb_ref, qg_ref, kg_ref, ones_ref,
                       b1_ref, b2_ref, hm_ref, kct_ref, vct_ref, sp_ref, sc_ref,
                       pw_ref, ps_ref, cw_ref, cbias_ref, wa_ref, wp_ref, wc_ref, wo_ref,
                       y_ref, ko_ref, vo_ref, po_ref, co_ref,
                       hbuf, qbuf, knbuf, vnbuf, obuf, pubuf, dbuf, cinbuf, ybuf, pext, cext):
    b = pl.program_id(0)
    T = t_new

    @pl.when(b == 0)
    def _():
        h = _rms(x_ref[...], g_ref[...]).astype(BF16)
        hbuf[...] = h
        q = _dot(h, win_ref[:, _Q0:_Q0 + ATTN_WIDTH])
        k = _dot(h, win_ref[:, _K0:_K0 + ATTN_WIDTH])
        v = _dot(h, win_ref[:, _V0:_V0 + ATTN_WIDTH])
        qn, kn = _head_rms2(q, k, qg_ref[...], kg_ref[...], ones_ref)
        qn = qn * ATTN_SCALE
        ko_ref[...] = kn
        vo_ref[...] = v
        qbuf[...] = qn.astype(BF16)
        knbuf[...] = kn.astype(BF16)
        vnbuf[...] = v.astype(BF16)
        pubuf[...] = _dot(h, win_ref[:, _P0:_P0 + POOL_WIDTH])
        cu = _dot(h, win_ref[:, _CU0:_CU0 + CONV_WIDTH])
        cc = _dot(h, win_ref[:, _CC0:_CC0 + CONV_WIDTH])
        cinbuf[...] = cc * cu

    for s in range(kct_ref.shape[0]):
        stream = b * kct_ref.shape[0] + s
        r0 = pl.multiple_of(stream * T, T)
        pext_s, cext_s = pext.at[s], cext.at[s]

        q_b = qbuf[pl.ds(r0, T), :]
        hm = hm_ref[...]
        qexp = jnp.where(hm > 0, jnp.concatenate([q_b] * N_HEADS, axis=0), jnp.zeros((), BF16))
        kct = kct_ref[s].astype(BF16)
        vct = vct_ref[s].astype(BF16)
        s1 = _dot(qexp, kct) + b1_ref[...]
        s2 = _dot_nt(qexp, knbuf[pl.ds(r0, T), :]) + b2_ref[...]
        m = jnp.maximum(jnp.max(s1, axis=-1, keepdims=True), jnp.max(s2, axis=-1, keepdims=True))
        p1 = jnp.exp(s1 - m)
        p2 = jnp.exp(s2 - m)
        denom = jnp.sum(p1, axis=-1, keepdims=True) + jnp.sum(p2, axis=-1, keepdims=True)
        oall = (_dot_nt(p1.astype(BF16), vct) + _dot(p2.astype(BF16), vnbuf[pl.ds(r0, T), :])) / denom
        oall = oall * hm
        o = oall[0:T, :]
        for hh in range(1, N_HEADS):
            o = o + oall[hh * T:(hh + 1) * T, :]
        obuf[pl.ds(r0, T), :] = o

        pu = pubuf[pl.ds(r0, T), :]
        pext_s[_POOL_PAD - POOL_HIST:_POOL_PAD, :] = sp_ref[stream]
        pext_s[_POOL_PAD:_POOL_PAD + T, :] = pu
        dbuf[pl.ds(r0, T), :] = _pool_delta(pext_s, pu, T, PAST_LEN)
        po_ref[stream] = pext_s[_POOL_PAD + T - POOL_HIST:_POOL_PAD + T, :]

        cext_s[_CONV_PAD - (CONV_K - 1):_CONV_PAD, :] = sc_ref[stream]
        cext_s[_CONV_PAD:_CONV_PAD + T, :] = cinbuf[pl.ds(r0, T), :]
        ybuf[pl.ds(r0, T), :] = _conv_taps(cext_s, T, cw_ref, cbias_ref)
        co_ref[stream] = cext_s[_CONV_PAD + T - (CONV_K - 1):_CONV_PAD + T, :]

    @pl.when(b == pl.num_programs(0) - 1)
    def _():
        h = hbuf[...]
        o_pool = _dot(dbuf[...].astype(BF16), pw_ref[...]) * ps_ref[...]
        o_conv = _dot(h, win_ref[:, _CB0:_CB0 + CONV_WIDTH]) * ybuf[...]
        side = _side_branches(h, o_pool, o_conv, win_ref, gb_ref, wp_ref, wc_ref)
        y_ref[...] = _project_out(x_ref[...], _gate(h, 0, win_ref, gb_ref), obuf[...], side,
                                  wa_ref, wo_ref)


def _mix_sample(x, cache_kt, cache_vt, state_pool, state_conv, layer, p):
    rows = x.shape[0]
    n_streams, r_cache = cache_kt.shape[1], cache_kt.shape[3]
    t_new = rows // n_streams
    in_w = p["mix_w_in"].shape[-1]
    cache_spec = pl.BlockSpec((None, _SAMPLE_STREAMS_PER_STEP, ATTN_WIDTH, r_cache),
                              lambda b: (layer, b, 0, 0))
    full = lambda shape: pl.BlockSpec(tuple(shape), lambda b: (0,) * len(shape))
    return pl.pallas_call(
        functools.partial(_mix_sample_kernel, t_new),
        grid=(n_streams // _SAMPLE_STREAMS_PER_STEP,),
        in_specs=[
            _shared_spec((rows, D_MODEL)),
            _const_spec((1, D_MODEL), layer),
            _const_spec((D_MODEL, in_w), layer),
            _const_spec((1, 3 * D_MODEL), layer),
            _const_spec((1, ATTN_WIDTH), layer),
            _const_spec((1, ATTN_WIDTH), layer),
            _shared_spec((ATTN_WIDTH, ATTN_WIDTH)),
            _const_spec((N_HEADS * t_new, r_cache), layer),
            _const_spec((N_HEADS * t_new, t_new), layer),
            _shared_spec((N_HEADS * t_new, ATTN_WIDTH)),
            cache_spec,
            cache_spec,
            _const_spec((n_streams, POOL_HIST, POOL_WIDTH), layer),
            _const_spec((n_streams, CONV_K - 1, CONV_WIDTH), layer),
            _const_spec((POOL_WIDTH, POOL_WIDTH), layer),
            _const_spec((1, POOL_WIDTH), layer),
            _const_spec((CONV_K, CONV_WIDTH), layer),
            _const_spec((1, CONV_WIDTH), layer),
            _const_spec((ATTN_WIDTH, D_MODEL), layer),
            _const_spec((POOL_WIDTH, D_MODEL), layer),
            _const_spec((CONV_WIDTH, D_MODEL), layer),
            _const_spec((D_MODEL, D_MODEL), layer),
        ],
        out_specs=[
            full((rows, D_MODEL)),
            full((rows, ATTN_WIDTH)),
            full((rows, ATTN_WIDTH)),
            full((n_streams, POOL_HIST, POOL_WIDTH)),
            full((n_streams, CONV_K - 1, CONV_WIDTH)),
        ],
        out_shape=[
            jax.ShapeDtypeStruct((rows, D_MODEL), F32),
            jax.ShapeDtypeStruct((rows, ATTN_WIDTH), F32),
            jax.ShapeDtypeStruct((rows, ATTN_WIDTH), F32),
            jax.ShapeDtypeStruct((n_streams, POOL_HIST, POOL_WIDTH), F32),
            jax.ShapeDtypeStruct((n_streams, CONV_K - 1, CONV_WIDTH), F32),
        ],
        scratch_shapes=[
            pltpu.VMEM((rows, D_MODEL), BF16),
            pltpu.VMEM((rows, ATTN_WIDTH), BF16),
            pltpu.VMEM((rows, ATTN_WIDTH), BF16),
            pltpu.VMEM((rows, ATTN_WIDTH), BF16),
            pltpu.VMEM((rows, ATTN_WIDTH), F32),
            pltpu.VMEM((rows, POOL_WIDTH), F32),
            pltpu.VMEM((rows, POOL_WIDTH), F32),
            pltpu.VMEM((rows, CONV_WIDTH), F32),
            pltpu.VMEM((rows, CONV_WIDTH), F32),
            pltpu.VMEM((_SAMPLE_STREAMS_PER_STEP, _POOL_PAD + t_new, POOL_WIDTH), F32),
            pltpu.VMEM((_SAMPLE_STREAMS_PER_STEP, _CONV_PAD + t_new, CONV_WIDTH), F32),
        ],
        compiler_params=pltpu.CompilerParams(
            dimension_semantics=("arbitrary",), vmem_limit_bytes=_VMEM_LIMIT_BYTES),
        name="mix_sample",
    )(x, p["mix_norm"], p["mix_w_in"], p["gate_b"], p["q_norm"], p["k_norm"], p["head_ones"],
      p["bias_cache"], p["bias_new"], p["head_mask"], cache_kt, cache_vt, state_pool, state_conv,
      p["pool_w"], p["pool_scale"], p["conv_w"], p["conv_b"],
      p["w_branch_attn"], p["w_branch_pool"], p["w_branch_conv"], p["w_out"])


def _rel_lookup(rel_bias, lo, hi):
    lead = rel_bias.shape[:-1]
    n_left = max(0, min(hi, -REL_CLIP) - lo)
    n_right = max(0, hi - max(lo, REL_CLIP + 1))
    a, b = max(lo, -REL_CLIP), min(hi, REL_CLIP + 1)
    parts = []
    if n_left:
        parts.append(jnp.broadcast_to(rel_bias[..., :1], lead + (n_left,)))
    if b > a:
        parts.append(rel_bias[..., a + REL_CLIP:b + REL_CLIP])
    if n_right:
        parts.append(jnp.broadcast_to(rel_bias[..., -1:], lead + (n_right,)))
    return jnp.concatenate(parts, axis=-1)


def _toeplitz(g, rows):
    n = g.shape[-1]
    lead = g.shape[:-1]
    t = jnp.broadcast_to(g[..., None, :], lead + (rows, n)).reshape(lead + (rows * n,))
    return t[..., :rows * (n - 1)].reshape(lead + (rows, n - 1))


def _band_bias(rel_bias, n_query, n_past, n_key):
    g = jnp.concatenate([_rel_lookup(rel_bias, -n_past, n_key - n_past),
                         _rel_lookup(rel_bias, -n_past - n_query, -n_past)], axis=-1)
    return _toeplitz(g, n_query)[..., :n_key].astype(F32)


def _prompt_bias(rel_bias):
    band = _band_bias(rel_bias, CHUNK, N_PREV_CHUNKS * CHUNK, BAND) * LOG2E
    masked = jnp.full(band.shape[:-1] + (KB - BAND,), NEG_INF, F32)
    return jnp.concatenate([jnp.concatenate([band, masked], axis=-1),
                            jnp.concatenate([masked, band], axis=-1)], axis=-2)


def _sample_bias(rel_bias, t_new, r_cache):
    depth = rel_bias.shape[0]
    table = _band_bias(rel_bias, t_new, r_cache, r_cache + t_new)
    table = table.reshape(depth, N_HEADS * t_new, r_cache + t_new)
    return table[..., :r_cache], table[..., r_cache:]


def kernel(x_prompt, x_sample, cache_attn_k, cache_attn_v, state_pool, state_conv, ffn1_norm, ffn1_w_in, ffn1_w_out, mix_norm, mix_w_in, gate_b, q_norm, k_norm, rel_bias, pool_w, pool_scale, conv_w, conv_b, w_branch_attn, w_branch_pool, w_branch_conv, w_out, ffn2_norm, ffn2_w_in, ffn2_w_out):
    batch, seq, _ = x_prompt.shape
    n_streams, t_new, _ = x_sample.shape
    depth = mix_w_in.shape[0]
    r_cache = cache_attn_k.shape[2]
    assert batch == 1 and seq % TM == 0 and seq >= TM
    assert r_cache == N_PREV_CHUNKS * CHUNK and PAST_LEN % CHUNK == 0 and t_new <= CHUNK
    assert n_streams % _SAMPLE_STREAMS_PER_STEP == 0

    row = lambda a: a.reshape(depth, 1, a.shape[-1])
    head = np.arange(ATTN_WIDTH) // HEAD_DIM
    n_groups = len(POOL_WINDOWS)
    pool_bd = jnp.einsum("lgcd,gh->lgchd", pool_w, jnp.eye(n_groups, dtype=pool_w.dtype))
    bias_cache, bias_new = _sample_bias(rel_bias, t_new, r_cache)
    p = dict(
        mix_norm=row(mix_norm), mix_w_in=mix_w_in.astype(BF16), gate_b=row(gate_b),
        q_norm=row(jnp.tile(q_norm, (1, N_HEADS))), k_norm=row(jnp.tile(k_norm, (1, N_HEADS))),
        head_ones=jnp.asarray(head[:, None] == head[None, :], BF16),
        bias_prompt=_prompt_bias(rel_bias), bias_cache=bias_cache, bias_new=bias_new,
        head_mask=jnp.asarray(np.repeat(np.arange(N_HEADS), t_new)[:, None] == head[None, :], F32),
        pool_w=pool_bd.reshape(depth, POOL_WIDTH, POOL_WIDTH).astype(BF16),
        pool_scale=row(pool_scale), conv_w=conv_w, conv_b=row(conv_b),
        w_branch_attn=w_branch_attn.astype(BF16), w_branch_pool=w_branch_pool.astype(BF16),
        w_branch_conv=w_branch_conv.astype(BF16), w_out=w_out.astype(BF16),
    )
    f1 = (row(ffn1_norm), ffn1_w_in.astype(BF16), ffn1_w_out.astype(BF16))
    f2 = (row(ffn2_norm), ffn2_w_in.astype(BF16), ffn2_w_out.astype(BF16))

    cache_kt = cache_attn_k.transpose(0, 1, 3, 4, 2).reshape(depth, n_streams, ATTN_WIDTH, r_cache)
    cache_vt = cache_attn_v.transpose(0, 1, 3, 4, 2).reshape(depth, n_streams, ATTN_WIDTH, r_cache)

    xp = x_prompt.reshape(seq, D_MODEL)
    xs = x_sample.reshape(n_streams * t_new, D_MODEL)
    outs = [[] for _ in range(8)]
    for l in range(depth):
        xp = _ffn(xp, *f1, l)
        xs = _ffn(xs, *f1, l)
        xp, kp, vp, pp, cp = _mix_prompt(xp, l, p)
        xs, ks, vs, ps, cs = _mix_sample(xs, cache_kt, cache_vt, state_pool, state_conv, l, p)
        xp = _ffn(xp, *f2, l)
        xs = _ffn(xs, *f2, l)
        for acc, val in zip(outs, (kp, vp, pp, cp, ks, vs, ps, cs)):
            acc.append(val)
    kp, vp, pp, cp, ks, vs, ps, cs = (jnp.stack(o) for o in outs)
    r_keep = min(N_PREV_CHUNKS * CHUNK, seq)
    return (xp.reshape(batch, seq, D_MODEL), xs.reshape(n_streams, t_new, D_MODEL),
            kp.reshape(depth, batch, r_keep, N_HEADS, HEAD_DIM),
            vp.reshape(depth, batch, r_keep, N_HEADS, HEAD_DIM),
            pp.reshape(depth, batch, POOL_HIST, POOL_WIDTH),
            cp.reshape(depth, batch, CONV_K - 1, CONV_WIDTH),
            ks.reshape(depth, n_streams, t_new, N_HEADS, HEAD_DIM),
            vs.reshape(depth, n_streams, t_new, N_HEADS, HEAD_DIM),
            ps, cs)
```

```python
import functools

import jax
import jax.numpy as jnp
import numpy as np
from jax import lax
from jax.experimental import pallas as pl
from jax.experimental.pallas import tpu as pltpu

D_MODEL = 1024
DEPTH = 4
PAST_LEN = 1024
CHUNK = 64
N_PREV_CHUNKS = 8
BAND = (N_PREV_CHUNKS + 1) * CHUNK
N_HEADS = 8
HEAD_DIM = 64
ATTN_WIDTH = N_HEADS * HEAD_DIM
ATTN_SCALE = HEAD_DIM ** -0.5
REL_CLIP = 128
POOL_WINDOWS = (2, 4, 8, 16)
POOL_GROUP = 64
POOL_WIDTH = len(POOL_WINDOWS) * POOL_GROUP
POOL_HIST = max(POOL_WINDOWS) - 1
CONV_WIDTH = 256
CONV_K = 3
D_FF = 2816
EPS = 1e-6
NEG_INF = -1e30

_Q0, _K0, _V0 = 0, ATTN_WIDTH, 2 * ATTN_WIDTH
_P0 = 3 * ATTN_WIDTH
_CU0 = _P0 + POOL_WIDTH
_CB0 = _CU0 + CONV_WIDTH
_CC0 = _CB0 + CONV_WIDTH
_G0 = _CC0 + CONV_WIDTH

TM = N_PREV_CHUNKS * CHUNK
QB = 2 * CHUNK
KB = BAND + CHUNK
QH = 2 * QB
KH = KB + QB
VW = 2 * HEAD_DIM
LOG2E = 1.4426950408889634
_FF_CHUNKS = ((0, 1024), (1024, 2048), (2048, D_FF))
_FFN_TM = 2 * TM
_POOL_PAD = 16
_CONV_PAD = 8
_SAMPLE_STREAMS_PER_STEP = 4

_V7X_VMEM_BYTES = 64 * 2 ** 20
_VMEM_UNSCOPED_BYTES = 8 * 2 ** 20
_VMEM_LIMIT_BYTES = _V7X_VMEM_BYTES - _VMEM_UNSCOPED_BYTES

F32 = jnp.float32
BF16 = jnp.bfloat16


def _dot(a, b):
    return jnp.dot(a, b, preferred_element_type=F32)


def _dot_nt(a, b):
    return lax.dot_general(a, b, (((1,), (1,)), ((), ())), preferred_element_type=F32)


def _rms(x, g):
    ms = jnp.mean(x * x, axis=-1, keepdims=True)
    return x * lax.rsqrt(ms + EPS) * g


def _head_rms2(q, k, qg, kg, head_ones_ref):
    rows = q.shape[0]
    sq = jnp.concatenate([(q * q).astype(BF16), (k * k).astype(BF16)], axis=0)
    ss = _dot(sq, head_ones_ref[...]) * (1.0 / HEAD_DIM)
    return (q * lax.rsqrt(ss[:rows] + EPS) * qg, k * lax.rsqrt(ss[rows:] + EPS) * kg)


def _pool_delta(ext_ref, pu, rows, pos0):
    def back(j):
        return ext_ref[_POOL_PAD - j:_POOL_PAD - j + rows, :]
    s2 = pu + back(1)
    s4 = s2 + back(2) + back(3)
    s8 = s4 + back(4) + back(5) + back(6) + back(7)
    s16 = s8
    for j in range(8, 16):
        s16 = s16 + back(j)
    lane = lax.broadcasted_iota(jnp.int32, (rows, POOL_WIDTH), 1)
    pos = lax.broadcasted_iota(jnp.int32, (rows, POOL_WIDTH), 0) + pos0
    g0, g1, g2 = lane < POOL_GROUP, lane < 2 * POOL_GROUP, lane < 3 * POOL_GROUP
    wsum = jnp.where(g0, s2, jnp.where(g1, s4, jnp.where(g2, s8, s16)))
    wlen = jnp.where(g0, 2, jnp.where(g1, 4, jnp.where(g2, 8, 16)))
    cnt = jnp.minimum(wlen, pos + 1).astype(F32)
    return wsum / cnt - pu


def _conv_taps(ext_ref, rows, cw_ref, cbias_ref):
    y = cbias_ref[...]
    for t in range(CONV_K):
        off = _CONV_PAD - (CONV_K - 1) + t
        y = y + ext_ref[off:off + rows, :] * cw_ref[t:t + 1, :]
    return y


def _gate(h, idx, win_ref, gb_ref):
    lo, hi = _G0 + idx * D_MODEL, _G0 + (idx + 1) * D_MODEL
    return jax.nn.sigmoid(_dot(h, win_ref[:, lo:hi]) + gb_ref[:, idx * D_MODEL:(idx + 1) * D_MODEL])


def _side_branches(h, o_pool, o_conv, win_ref, gb_ref, wp_ref, wc_ref):
    return (_gate(h, 1, win_ref, gb_ref) * _dot(o_pool.astype(BF16), wp_ref[...])
            + _gate(h, 2, win_ref, gb_ref) * _dot(o_conv.astype(BF16), wc_ref[...]))


def _project_out(x, attn_gate, o_attn, side, wa_ref, wo_ref):
    m = attn_gate * _dot(o_attn.astype(BF16), wa_ref[...]) + side
    return x + _dot(m.astype(BF16), wo_ref[...])


def _ffn_kernel(layer, x_ref, g_ref, win_hbm, wout_hbm, o_ref, win_v, wout_v, sems):
    def weight_copies(c):
        a, b = _FF_CHUNKS[c]
        return (
            pltpu.make_async_copy(win_hbm.at[layer, :, a:b], win_v.at[:, a:b], sems.at[3 * c]),
            pltpu.make_async_copy(win_hbm.at[layer, :, D_FF + a:D_FF + b],
                                  win_v.at[:, D_FF + a:D_FF + b], sems.at[3 * c + 1]),
            pltpu.make_async_copy(wout_hbm.at[layer, a:b, :], wout_v.at[a:b, :], sems.at[3 * c + 2]),
        )

    def tile(load_weights):
        if load_weights:
            for c in range(len(_FF_CHUNKS)):
                for cp in weight_copies(c):
                    cp.start()
        x = x_ref[...]
        h = _rms(x, g_ref[...]).astype(BF16)
        acc = None
        for c, (a, b) in enumerate(_FF_CHUNKS):
            if load_weights:
                for cp in weight_copies(c):
                    cp.wait()
            gate = _dot(h, win_v[:, a:b])
            up = _dot(h, win_v[:, D_FF + a:D_FF + b])
            act = (gate * jax.nn.sigmoid(gate) * up).astype(BF16)
            part = _dot(act, wout_v[a:b, :])
            acc = part if acc is None else acc + part
        o_ref[...] = x + 0.5 * acc

    i = pl.program_id(0)

    @pl.when(i == 0)
    def _():
        tile(load_weights=True)

    @pl.when(i > 0)
    def _():
        tile(load_weights=False)


def _const_spec(shape, layer):
    nd = len(shape)
    return pl.BlockSpec((None,) + tuple(shape), lambda i: (layer,) + (0,) * nd,
                        pipeline_mode=pl.Buffered(1))


def _shared_spec(shape):
    nd = len(shape)
    return pl.BlockSpec(tuple(shape), lambda i: (0,) * nd, pipeline_mode=pl.Buffered(1))


def _ffn(x, norm, w_in, w_out, layer):
    rows = x.shape[0]
    tm = min(_FFN_TM, rows)
    return pl.pallas_call(
        functools.partial(_ffn_kernel, layer),
        grid=(rows // tm,),
        in_specs=[
            pl.BlockSpec((tm, D_MODEL), lambda i: (i, 0)),
            _const_spec((1, D_MODEL), layer),
            pl.BlockSpec(memory_space=pl.ANY),
            pl.BlockSpec(memory_space=pl.ANY),
        ],
        out_specs=pl.BlockSpec((tm, D_MODEL), lambda i: (i, 0)),
        out_shape=jax.ShapeDtypeStruct((rows, D_MODEL), F32),
        scratch_shapes=[
            pltpu.VMEM((D_MODEL, 2 * D_FF), BF16),
            pltpu.VMEM((D_FF, D_MODEL), BF16),
            pltpu.SemaphoreType.DMA((3 * len(_FF_CHUNKS),)),
        ],
        compiler_params=pltpu.CompilerParams(
            dimension_semantics=("arbitrary",), vmem_limit_bytes=_VMEM_LIMIT_BYTES),
        name="ffn",
    )(x, norm, w_in, w_out)


def _band_attention(i, qbuf, kbuf, vbuf, bias_ref, obuf, s_bufs, p_bufs):
    n_items = (TM // QH) * N_HEADS
    first_valid = jnp.where(i > 0, 0, TM)
    col = lax.broadcasted_iota(jnp.int32, (QB, KB), 1)

    def item(n):
        return n % N_HEADS, (n // N_HEADS) * QH

    def scores(n, s_ref):
        hh, r0 = item(n)
        s_ref[...] = _dot_nt(qbuf[hh, pl.ds(r0, QH), :], kbuf[hh, pl.ds(r0, KH), :])

    def softmax(n, s_ref, p_ref):
        hh, r0 = item(n)
        for a in range(QH // QB):
            sa = s_ref[a * QB:(a + 1) * QB, a * QB:a * QB + KB] + bias_ref[hh]
            sa = jnp.where(col >= first_valid - r0 - a * QB, sa, NEG_INF)
            p_ref[a * QB:(a + 1) * QB, a * QB:a * QB + KB] = jnp.exp2(
                sa - jnp.max(sa, axis=-1, keepdims=True)).astype(BF16)

    def values(n, p_ref):
        hh, r0 = item(n)
        o = _dot(p_ref[...], vbuf[hh, pl.ds(r0, KH), :])
        obuf[hh, pl.ds(r0, QH), :] = o[:, :HEAD_DIM] / o[:, HEAD_DIM:HEAD_DIM + 1]

    s0, s1 = s_bufs
    p0, p1 = p_bufs
    scores(0, s0)
    softmax(0, s0, p0)
    scores(1, s1)

    def group(k, s_cur, p_cur, s_nxt, p_prev):
        values(k - 1, p_prev)
        softmax(k, s_cur, p_cur)
        scores(k + 1, s_nxt)

    for k in range(1, n_items - 1):
        if k % 2:
            group(k, s1, p1, s0, p0)
        else:
            group(k, s0, p0, s1, p1)
    values(n_items - 2, p0)
    softmax(n_items - 1, s1, p1)
    values(n_items - 1, p1)


def _mix_prompt_kernel(x_ref, g_ref, win_ref, gb_ref, qg_ref, kg_ref, ones_ref, bias_ref,
                       pw_ref, ps_ref, cw_ref, cbias_ref, wa_ref, wp_ref, wc_ref, wo_ref,
                       y_ref, ko_ref, vo_ref, po_ref, co_ref,
                       qbuf, kbuf, vbuf, obuf, s0buf, s1buf, p0buf, p1buf, pext, cext):
    i = pl.program_id(0)

    @pl.when(i == 0)
    def _():
        kbuf[:, 0:TM, :] = jnp.zeros((N_HEADS, TM, HEAD_DIM), BF16)
        ones_col = lax.broadcasted_iota(jnp.int32, (N_HEADS, 2 * TM, VW), 2) == HEAD_DIM
        vbuf[...] = jnp.where(ones_col, 1.0, 0.0).astype(BF16)
        pext[0:_POOL_PAD, :] = jnp.zeros((_POOL_PAD, POOL_WIDTH), F32)
        cext[0:_CONV_PAD, :] = jnp.zeros((_CONV_PAD, CONV_WIDTH), F32)
        p0buf[...] = jnp.zeros((QH, KH), BF16)
        p1buf[...] = jnp.zeros((QH, KH), BF16)

    x = x_ref[...]
    h = _rms(x, g_ref[...]).astype(BF16)

    qkv = _dot(h, win_ref[:, _Q0:_P0])
    q = qkv[:, _Q0:_K0]
    k = qkv[:, _K0:_V0]
    v = qkv[:, _V0:_P0]
    qn, kn = _head_rms2(q, k, qg_ref[...], kg_ref[...], ones_ref)
    qn = qn * (ATTN_SCALE * LOG2E)
    ko_ref[...] = kn
    vo_ref[...] = v
    for hh in range(N_HEADS):
        sl = slice(hh * HEAD_DIM, (hh + 1) * HEAD_DIM)
        qbuf[hh] = qn[:, sl].astype(BF16)
        kbuf[hh, TM:2 * TM, :] = kn[:, sl].astype(BF16)
        vbuf[hh, TM:2 * TM, 0:HEAD_DIM] = v[:, sl].astype(BF16)

    side_in = _dot(h, win_ref[:, _P0:_G0])
    pu = side_in[:, 0:POOL_WIDTH]
    pext[_POOL_PAD:_POOL_PAD + TM, :] = pu
    d = _pool_delta(pext, pu, TM, i * TM)
    o_pool = _dot(d.astype(BF16), pw_ref[...]) * ps_ref[...]
    po_ref[...] = pext[_POOL_PAD + TM - POOL_HIST:_POOL_PAD + TM, :]
    pext[0:_POOL_PAD, :] = pext[TM:TM + _POOL_PAD, :]

    cu = side_in[:, _CU0 - _P0:_CB0 - _P0]
    cgate = side_in[:, _CB0 - _P0:_CC0 - _P0]
    cc = side_in[:, _CC0 - _P0:_G0 - _P0]
    cext[_CONV_PAD:_CONV_PAD + TM, :] = cc * cu
    o_conv = cgate * _conv_taps(cext, TM, cw_ref, cbias_ref)
    co_ref[...] = cext[_CONV_PAD + TM - (CONV_K - 1):_CONV_PAD + TM, :]
    cext[0:_CONV_PAD, :] = cext[TM:TM + _CONV_PAD, :]

    side = _side_branches(h, o_pool, o_conv, win_ref, gb_ref, wp_ref, wc_ref)
    attn_gate = _gate(h, 0, win_ref, gb_ref)

    _band_attention(i, qbuf, kbuf, vbuf, bias_ref, obuf, (s0buf, s1buf), (p0buf, p1buf))

    o_attn = jnp.concatenate([obuf[hh] for hh in range(N_HEADS)], axis=1)
    y_ref[...] = _project_out(x, attn_gate, o_attn, side, wa_ref, wo_ref)

    kbuf[:, 0:TM, :] = kbuf[:, TM:2 * TM, :]
    vbuf[:, 0:TM, :] = vbuf[:, TM:2 * TM, :]


def _mix_prompt(x, layer, p):
    rows = x.shape[0]
    in_w = p["mix_w_in"].shape[-1]
    return pl.pallas_call(
        _mix_prompt_kernel,
        grid=(rows // TM,),
        in_specs=[
            pl.BlockSpec((TM, D_MODEL), lambda i: (i, 0)),
            _const_spec((1, D_MODEL), layer),
            _const_spec((D_MODEL, in_w), layer),
            _const_spec((1, 3 * D_MODEL), layer),
            _const_spec((1, ATTN_WIDTH), layer),
            _const_spec((1, ATTN_WIDTH), layer),
            _shared_spec((ATTN_WIDTH, ATTN_WIDTH)),
            _const_spec((N_HEADS, QB, KB), layer),
            _const_spec((POOL_WIDTH, POOL_WIDTH), layer),
            _const_spec((1, POOL_WIDTH), layer),
            _const_spec((CONV_K, CONV_WIDTH), layer),
            _const_spec((1, CONV_WIDTH), layer),
            _const_spec((ATTN_WIDTH, D_MODEL), layer),
            _const_spec((POOL_WIDTH, D_MODEL), layer),
            _const_spec((CONV_WIDTH, D_MODEL), layer),
            _const_spec((D_MODEL, D_MODEL), layer),
        ],
        out_specs=[
            pl.BlockSpec((TM, D_MODEL), lambda i: (i, 0)),
            pl.BlockSpec((TM, ATTN_WIDTH), lambda i: (0, 0)),
            pl.BlockSpec((TM, ATTN_WIDTH), lambda i: (0, 0)),
            pl.BlockSpec((POOL_HIST, POOL_WIDTH), lambda i: (0, 0)),
            pl.BlockSpec((CONV_K - 1, CONV_WIDTH), lambda i: (0, 0)),
        ],
        out_shape=[
            jax.ShapeDtypeStruct((rows, D_MODEL), F32),
            jax.ShapeDtypeStruct((TM, ATTN_WIDTH), F32),
            jax.ShapeDtypeStruct((TM, ATTN_WIDTH), F32),
            jax.ShapeDtypeStruct((POOL_HIST, POOL_WIDTH), F32),
            jax.ShapeDtypeStruct((CONV_K - 1, CONV_WIDTH), F32),
        ],
        scratch_shapes=[
            pltpu.VMEM((N_HEADS, TM, HEAD_DIM), BF16),
            pltpu.VMEM((N_HEADS, 2 * TM, HEAD_DIM), BF16),
            pltpu.VMEM((N_HEADS, 2 * TM, VW), BF16),
            pltpu.VMEM((N_HEADS, TM, HEAD_DIM), F32),
            pltpu.VMEM((QH, KH), F32),
            pltpu.VMEM((QH, KH), F32),
            pltpu.VMEM((QH, KH), BF16),
            pltpu.VMEM((QH, KH), BF16),
            pltpu.VMEM((_POOL_PAD + TM, POOL_WIDTH), F32),
            pltpu.VMEM((_CONV_PAD + TM, CONV_WIDTH), F32),
        ],
        compiler_params=pltpu.CompilerParams(
            dimension_semantics=("arbitrary",), vmem_limit_bytes=_VMEM_LIMIT_BYTES),
        name="mix_prompt",
    )(x, p["mix_norm"], p["mix_w_in"], p["gate_b"], p["q_norm"], p["k_norm"], p["head_ones"],
      p["bias_prompt"], p["pool_w"], p["pool_scale"], p["conv_w"], p["conv_b"],
      p["w_branch_attn"], p["w_branch_pool"], p["w_branch_conv"], p["w_out"])


def _mix_sample_kernel(t_new, x_ref, g_ref, win_ref, gb_ref, qg_ref, kg_ref, ones_ref,
                       b1_ref, b2_ref, hm_ref, kct_ref, vct_ref, sp_ref, sc_ref,
                       pw_ref, ps_ref, cw_ref, cbias_ref, wa_ref, wp_ref, wc_ref, wo_ref,
                       y_ref, ko_ref, vo_ref, po_ref, co_ref,
                       hbuf, qbuf, knbuf, vnbuf, obuf, pubuf, dbuf, cinbuf, ybuf, pext, cext):
    b = pl.program_id(0)
    T = t_new

    @pl.when(b == 0)
    def _():
        h = _rms(x_ref[...], g_ref[...]).astype(BF16)
        hbuf[...] = h
        q = _dot(h, win_ref[:, _Q0:_Q0 + ATTN_WIDTH])
        k = _dot(h, win_ref[:, _K0:_K0 + ATTN_WIDTH])
        v = _dot(h, win_ref[:, _V0:_V0 + ATTN_WIDTH])
        qn, kn = _head_rms2(q, k, qg_ref[...], kg_ref[...], ones_ref)
        qn = qn * ATTN_SCALE
        ko_ref[...] = kn
        vo_ref[...] = v
        qbuf[...] = qn.astype(BF16)
        knbuf[...] = kn.astype(BF16)
        vnbuf[...] = v.astype(BF16)
        pubuf[...] = _dot(h, win_ref[:, _P0:_P0 + POOL_WIDTH])
        cu = _dot(h, win_ref[:, _CU0:_CU0 + CONV_WIDTH])
        cc = _dot(h, win_ref[:, _CC0:_CC0 + CONV_WIDTH])
        cinbuf[...] = cc * cu

    for s in range(kct_ref.shape[0]):
        stream = b * kct_ref.shape[0] + s
        r0 = pl.multiple_of(stream * T, T)
        pext_s, cext_s = pext.at[s], cext.at[s]

        q_b = qbuf[pl.ds(r0, T), :]
        hm = hm_ref[...]
        qexp = jnp.where(hm > 0, jnp.concatenate([q_b] * N_HEADS, axis=0), jnp.zeros((), BF16))
        kct = kct_ref[s].astype(BF16)
        vct = vct_ref[s].astype(BF16)
        s1 = _dot(qexp, kct) + b1_ref[...]
        s2 = _dot_nt(qexp, knbuf[pl.ds(r0, T), :]) + b2_ref[...]
        m = jnp.maximum(jnp.max(s1, axis=-1, keepdims=True), jnp.max(s2, axis=-1, keepdims=True))
        p1 = jnp.exp(s1 - m)
        p2 = jnp.exp(s2 - m)
        denom = jnp.sum(p1, axis=-1, keepdims=True) + jnp.sum(p2, axis=-1, keepdims=True)
        oall = (_dot_nt(p1.astype(BF16), vct) + _dot(p2.astype(BF16), vnbuf[pl.ds(r0, T), :])) / denom
        oall = oall * hm
        o = oall[0:T, :]
        for hh in range(1, N_HEADS):
            o = o + oall[hh * T:(hh + 1) * T, :]
        obuf[pl.ds(r0, T), :] = o

        pu = pubuf[pl.ds(r0, T), :]
        pext_s[_POOL_PAD - POOL_HIST:_POOL_PAD, :] = sp_ref[stream]
        pext_s[_POOL_PAD:_POOL_PAD + T, :] = pu
        dbuf[pl.ds(r0, T), :] = _pool_delta(pext_s, pu, T, PAST_LEN)
        po_ref[stream] = pext_s[_POOL_PAD + T - POOL_HIST:_POOL_PAD + T, :]

        cext_s[_CONV_PAD - (CONV_K - 1):_CONV_PAD, :] = sc_ref[stream]
        cext_s[_CONV_PAD:_CONV_PAD + T, :] = cinbuf[pl.ds(r0, T), :]
        ybuf[pl.ds(r0, T), :] = _conv_taps(cext_s, T, cw_ref, cbias_ref)
        co_ref[stream] = cext_s[_CONV_PAD + T - (CONV_K - 1):_CONV_PAD + T, :]

    @pl.when(b == pl.num_programs(0) - 1)
    def _():
        h = hbuf[...]
        o_pool = _dot(dbuf[...].astype(BF16), pw_ref[...]) * ps_ref[...]
        o_conv = _dot(h, win_ref[:, _CB0:_CB0 + CONV_WIDTH]) * ybuf[...]
        side = _side_branches(h, o_pool, o_conv, win_ref, gb_ref, wp_ref, wc_ref)
        y_ref[...] = _project_out(x_ref[...], _gate(h, 0, win_ref, gb_ref), obuf[...], side,
                                  wa_ref, wo_ref)


def _mix_sample(x, cache_kt, cache_vt, state_pool, state_conv, layer, p):
    rows = x.shape[0]
    n_streams, r_cache = cache_kt.shape[1], cache_kt.shape[3]
    t_new = rows // n_streams
    in_w = p["mix_w_in"].shape[-1]
    cache_spec = pl.BlockSpec((None, _SAMPLE_STREAMS_PER_STEP, ATTN_WIDTH, r_cache),
                              lambda b: (layer, b, 0, 0))
    full = lambda shape: pl.BlockSpec(tuple(shape), lambda b: (0,) * len(shape))
    return pl.pallas_call(
        functools.partial(_mix_sample_kernel, t_new),
        grid=(n_streams // _SAMPLE_STREAMS_PER_STEP,),
        in_specs=[
            _shared_spec((rows, D_MODEL)),
            _const_spec((1, D_MODEL), layer),
            _const_spec((D_MODEL, in_w), layer),
            _const_spec((1, 3 * D_MODEL), layer),
            _const_spec((1, ATTN_WIDTH), layer),
            _const_spec((1, ATTN_WIDTH), layer),
            _shared_spec((ATTN_WIDTH, ATTN_WIDTH)),
            _const_spec((N_HEADS * t_new, r_cache), layer),
            _const_spec((N_HEADS * t_new, t_new), layer),
            _shared_spec((N_HEADS * t_new, ATTN_WIDTH)),
            cache_spec,
            cache_spec,
            _const_spec((n_streams, POOL_HIST, POOL_WIDTH), layer),
            _const_spec((n_streams, CONV_K - 1, CONV_WIDTH), layer),
            _const_spec((POOL_WIDTH, POOL_WIDTH), layer),
            _const_spec((1, POOL_WIDTH), layer),
            _const_spec((CONV_K, CONV_WIDTH), layer),
            _const_spec((1, CONV_WIDTH), layer),
            _const_spec((ATTN_WIDTH, D_MODEL), layer),
            _const_spec((POOL_WIDTH, D_MODEL), layer),
            _const_spec((CONV_WIDTH, D_MODEL), layer),
            _const_spec((D_MODEL, D_MODEL), layer),
        ],
        out_specs=[
            full((rows, D_MODEL)),
            full((rows, ATTN_WIDTH)),
            full((rows, ATTN_WIDTH)),
            full((n_streams, POOL_HIST, POOL_WIDTH)),
            full((n_streams, CONV_K - 1, CONV_WIDTH)),
        ],
        out_shape=[
            jax.ShapeDtypeStruct((rows, D_MODEL), F32),
            jax.ShapeDtypeStruct((rows, ATTN_WIDTH), F32),
            jax.ShapeDtypeStruct((rows, ATTN_WIDTH), F32),
            jax.ShapeDtypeStruct((n_streams, POOL_HIST, POOL_WIDTH), F32),
            jax.ShapeDtypeStruct((n_streams, CONV_K - 1, CONV_WIDTH), F32),
        ],
        scratch_shapes=[
            pltpu.VMEM((rows, D_MODEL), BF16),
            pltpu.VMEM((rows, ATTN_WIDTH), BF16),
            pltpu.VMEM((rows, ATTN_WIDTH), BF16),
            pltpu.VMEM((rows, ATTN_WIDTH), BF16),
            pltpu.VMEM((rows, ATTN_WIDTH), F32),
            pltpu.VMEM((rows, POOL_WIDTH), F32),
            pltpu.VMEM((rows, POOL_WIDTH), F32),
            pltpu.VMEM((rows, CONV_WIDTH), F32),
            pltpu.VMEM((rows, CONV_WIDTH), F32),
            pltpu.VMEM((_SAMPLE_STREAMS_PER_STEP, _POOL_PAD + t_new, POOL_WIDTH), F32),
            pltpu.VMEM((_SAMPLE_STREAMS_PER_STEP, _CONV_PAD + t_new, CONV_WIDTH), F32),
        ],
        compiler_params=pltpu.CompilerParams(
            dimension_semantics=("arbitrary",), vmem_limit_bytes=_VMEM_LIMIT_BYTES),
        name="mix_sample",
    )(x, p["mix_norm"], p["mix_w_in"], p["gate_b"], p["q_norm"], p["k_norm"], p["head_ones"],
      p["bias_cache"], p["bias_new"], p["head_mask"], cache_kt, cache_vt, state_pool, state_conv,
      p["pool_w"], p["pool_scale"], p["conv_w"], p["conv_b"],
      p["w_branch_attn"], p["w_branch_pool"], p["w_branch_conv"], p["w_out"])


def _rel_lookup(rel_bias, lo, hi):
    lead = rel_bias.shape[:-1]
    n_left = max(0, min(hi, -REL_CLIP) - lo)
    n_right = max(0, hi - max(lo, REL_CLIP + 1))
    a, b = max(lo, -REL_CLIP), min(hi, REL_CLIP + 1)
    parts = []
    if n_left:
        parts.append(jnp.broadcast_to(rel_bias[..., :1], lead + (n_left,)))
    if b > a:
        parts.append(rel_bias[..., a + REL_CLIP:b + REL_CLIP])
    if n_right:
        parts.append(jnp.broadcast_to(rel_bias[..., -1:], lead + (n_right,)))
    return jnp.concatenate(parts, axis=-1)


def _toeplitz(g, rows):
    n = g.shape[-1]
    lead = g.shape[:-1]
    t = jnp.broadcast_to(g[..., None, :], lead + (rows, n)).reshape(lead + (rows * n,))
    return t[..., :rows * (n - 1)].reshape(lead + (rows, n - 1))


def _band_bias(rel_bias, n_query, n_past, n_key):
    g = jnp.concatenate([_rel_lookup(rel_bias, -n_past, n_key - n_past),
                         _rel_lookup(rel_bias, -n_past - n_query, -n_past)], axis=-1)
    return _toeplitz(g, n_query)[..., :n_key].astype(F32)


def _prompt_bias(rel_bias):
    band = _band_bias(rel_bias, CHUNK, N_PREV_CHUNKS * CHUNK, BAND) * LOG2E
    masked = jnp.full(band.shape[:-1] + (KB - BAND,), NEG_INF, F32)
    return jnp.concatenate([jnp.concatenate([band, masked], axis=-1),
                            jnp.concatenate([masked, band], axis=-1)], axis=-2)


def _sample_bias(rel_bias, t_new, r_cache):
    depth = rel_bias.shape[0]
    table = _band_bias(rel_bias, t_new, r_cache, r_cache + t_new)
    table = table.reshape(depth, N_HEADS * t_new, r_cache + t_new)
    return table[..., :r_cache], table[..., r_cache:]


def kernel(x_prompt, x_sample, cache_attn_k, cache_attn_v, state_pool, state_conv, ffn1_norm, ffn1_w_in, ffn1_w_out, mix_norm, mix_w_in, gate_b, q_norm, k_norm, rel_bias, pool_w, pool_scale, conv_w, conv_b, w_branch_attn, w_branch_pool, w_branch_conv, w_out, ffn2_norm, ffn2_w_in, ffn2_w_out):
    batch, seq, _ = x_prompt.shape
    n_streams, t_new, _ = x_sample.shape
    depth = mix_w_in.shape[0]
    r_cache = cache_attn_k.shape[2]
    assert batch == 1 and seq % TM == 0 and seq >= TM
    assert r_cache == N_PREV_CHUNKS * CHUNK and PAST_LEN % CHUNK == 0 and t_new <= CHUNK
    assert n_streams % _SAMPLE_STREAMS_PER_STEP == 0

    row = lambda a: a.reshape(depth, 1, a.shape[-1])
    head = np.arange(ATTN_WIDTH) // HEAD_DIM
    n_groups = len(POOL_WINDOWS)
    pool_bd = jnp.einsum("lgcd,gh->lgchd", pool_w, jnp.eye(n_groups, dtype=pool_w.dtype))
    bias_cache, bias_new = _sample_bias(rel_bias, t_new, r_cache)
    p = dict(
        mix_norm=row(mix_norm), mix_w_in=mix_w_in.astype(BF16), gate_b=row(gate_b),
        q_norm=row(jnp.tile(q_norm, (1, N_HEADS))), k_norm=row(jnp.tile(k_norm, (1, N_HEADS))),
        head_ones=jnp.asarray(head[:, None] == head[None, :], BF16),
        bias_prompt=_prompt_bias(rel_bias), bias_cache=bias_cache, bias_new=bias_new,
        head_mask=jnp.asarray(np.repeat(np.arange(N_HEADS), t_new)[:, None] == head[None, :], F32),
        pool_w=pool_bd.reshape(depth, POOL_WIDTH, POOL_WIDTH).astype(BF16),
        pool_scale=row(pool_scale), conv_w=conv_w, conv_b=row(conv_b),
        w_branch_attn=w_branch_attn.astype(BF16), w_branch_pool=w_branch_pool.astype(BF16),
        w_branch_conv=w_branch_conv.astype(BF16), w_out=w_out.astype(BF16),
    )
    f1 = (row(ffn1_norm), ffn1_w_in.astype(BF16), ffn1_w_out.astype(BF16))
    f2 = (row(ffn2_norm), ffn2_w_in.astype(BF16), ffn2_w_out.astype(BF16))

    cache_kt = cache_attn_k.transpose(0, 1, 3, 4, 2).reshape(depth, n_streams, ATTN_WIDTH, r_cache)
    cache_vt = cache_attn_v.transpose(0, 1, 3, 4, 2).reshape(depth, n_streams, ATTN_WIDTH, r_cache)

    xp = x_prompt.reshape(seq, D_MODEL)
    xs = x_sample.reshape(n_streams * t_new, D_MODEL)
    outs = [[] for _ in range(8)]
    for l in range(depth):
        xp = _ffn(xp, *f1, l)
        xs = _ffn(xs, *f1, l)
        xp, kp, vp, pp, cp = _mix_prompt(xp, l, p)
        xs, ks, vs, ps, cs = _mix_sample(xs, cache_kt, cache_vt, state_pool, state_conv, l, p)
        xp = _ffn(xp, *f2, l)
        xs = _ffn(xs, *f2, l)
        for acc, val in zip(outs, (kp, vp, pp, cp, ks, vs, ps, cs)):
            acc.append(val)
    kp, vp, pp, cp, ks, vs, ps, cs = (jnp.stack(o) for o in outs)
    r_keep = min(N_PREV_CHUNKS * CHUNK, seq)
    return (xp.reshape(batch, seq, D_MODEL), xs.reshape(n_streams, t_new, D_MODEL),
            kp.reshape(depth, batch, r_keep, N_HEADS, HEAD_DIM),
            vp.reshape(depth, batch, r_keep, N_HEADS, HEAD_DIM),
            pp.reshape(depth, batch, POOL_HIST, POOL_WIDTH),
            cp.reshape(depth, batch, CONV_K - 1, CONV_WIDTH),
            ks.reshape(depth, n_streams, t_new, N_HEADS, HEAD_DIM),
            vs.reshape(depth, n_streams, t_new, N_HEADS, HEAD_DIM),
            ps, cs)
```

```python
import functools

import jax
import jax.numpy as jnp
import numpy as np
from jax import lax
from jax.experimental import pallas as pl
from jax.experimental.pallas import tpu as pltpu

D_MODEL = 1024
DEPTH = 4
PAST_LEN = 1024
CHUNK = 64
N_PREV_CHUNKS = 8
BAND = (N_PREV_CHUNKS + 1) * CHUNK
N_HEADS = 8
HEAD_DIM = 64
ATTN_WIDTH = N_HEADS * HEAD_DIM
ATTN_SCALE = HEAD_DIM ** -0.5
REL_CLIP = 128
POOL_WINDOWS = (2, 4, 8, 16)
POOL_GROUP = 64
POOL_WIDTH = len(POOL_WINDOWS) * POOL_GROUP
POOL_HIST = max(POOL_WINDOWS) - 1
CONV_WIDTH = 256
CONV_K = 3
D_FF = 2816
EPS = 1e-6
NEG_INF = -1e30

_Q0, _K0, _V0 = 0, ATTN_WIDTH, 2 * ATTN_WIDTH
_P0 = 3 * ATTN_WIDTH
_CU0 = _P0 + POOL_WIDTH
_CB0 = _CU0 + CONV_WIDTH
_CC0 = _CB0 + CONV_WIDTH
_G0 = _CC0 + CONV_WIDTH

TM = N_PREV_CHUNKS * CHUNK
QB = 2 * CHUNK
KB = BAND + CHUNK
QH = 2 * QB
KH = KB + QB
VW = 2 * HEAD_DIM
LOG2E = 1.4426950408889634
_FF_CHUNKS = ((0, 1024), (1024, 2048), (2048, D_FF))
_FFN_TM = 2 * TM
_POOL_PAD = 16
_CONV_PAD = 8
_SAMPLE_STREAMS_PER_STEP = 4

_V7X_VMEM_BYTES = 64 * 2 ** 20
_VMEM_UNSCOPED_BYTES = 8 * 2 ** 20
_VMEM_LIMIT_BYTES = _V7X_VMEM_BYTES - _VMEM_UNSCOPED_BYTES

F32 = jnp.float32
BF16 = jnp.bfloat16


def _dot(a, b):
    return jnp.dot(a, b, preferred_element_type=F32)


def _dot_nt(a, b):
    return lax.dot_general(a, b, (((1,), (1,)), ((), ())), preferred_element_type=F32)


def _rms(x, g):
    ms = jnp.mean(x * x, axis=-1, keepdims=True)
    return x * lax.rsqrt(ms + EPS) * g


def _head_rms2(q, k, qg, kg, head_ones_ref):
    rows = q.shape[0]
    sq = jnp.concatenate([(q * q).astype(BF16), (k * k).astype(BF16)], axis=0)
    ss = _dot(sq, head_ones_ref[...]) * (1.0 / HEAD_DIM)
    return (q * lax.rsqrt(ss[:rows] + EPS) * qg, k * lax.rsqrt(ss[rows:] + EPS) * kg)


def _pool_delta(ext_ref, pu, rows, pos0):
    def back(j):
        return ext_ref[_POOL_PAD - j:_POOL_PAD - j + rows, :]
    s2 = pu + back(1)
    s4 = s2 + back(2) + back(3)
    s8 = s4 + back(4) + back(5) + back(6) + back(7)
    s16 = s8
    for j in range(8, 16):
        s16 = s16 + back(j)
    lane = lax.broadcasted_iota(jnp.int32, (rows, POOL_WIDTH), 1)
    pos = lax.broadcasted_iota(jnp.int32, (rows, POOL_WIDTH), 0) + pos0
    g0, g1, g2 = lane < POOL_GROUP, lane < 2 * POOL_GROUP, lane < 3 * POOL_GROUP
    wsum = jnp.where(g0, s2, jnp.where(g1, s4, jnp.where(g2, s8, s16)))
    wlen = jnp.where(g0, 2, jnp.where(g1, 4, jnp.where(g2, 8, 16)))
    cnt = jnp.minimum(wlen, pos + 1).astype(F32)
    return wsum / cnt - pu


def _conv_taps(ext_ref, rows, cw_ref, cbias_ref):
    y = cbias_ref[...]
    for t in range(CONV_K):
        off = _CONV_PAD - (CONV_K - 1) + t
        y = y + ext_ref[off:off + rows, :] * cw_ref[t:t + 1, :]
    return y


def _gate(h, idx, win_ref, gb_ref):
    lo, hi = _G0 + idx * D_MODEL, _G0 + (idx + 1) * D_MODEL
    return jax.nn.sigmoid(_dot(h, win_ref[:, lo:hi]) + gb_ref[:, idx * D_MODEL:(idx + 1) * D_MODEL])


def _side_branches(h, o_pool, o_conv, win_ref, gb_ref, wp_ref, wc_ref):
    return (_gate(h, 1, win_ref, gb_ref) * _dot(o_pool.astype(BF16), wp_ref[...])
            + _gate(h, 2, win_ref, gb_ref) * _dot(o_conv.astype(BF16), wc_ref[...]))


def _project_out(x, attn_gate, o_attn, side, wa_ref, wo_ref):
    m = attn_gate * _dot(o_attn.astype(BF16), wa_ref[...]) + side
    return x + _dot(m.astype(BF16), wo_ref[...])


def _ffn_kernel(x_ref, g_ref, win_ref, wout_ref, o_ref):
    x = x_ref[...]
    h = _rms(x, g_ref[...]).astype(BF16)
    acc = None
    for a, b in _FF_CHUNKS:
        gate = _dot(h, win_ref[:, a:b])
        up = _dot(h, win_ref[:, D_FF + a:D_FF + b])
        act = (gate * jax.nn.sigmoid(gate) * up).astype(BF16)
        part = _dot(act, wout_ref[a:b, :])
        acc = part if acc is None else acc + part
    o_ref[...] = x + 0.5 * acc


def _const_spec(shape, layer):
    nd = len(shape)
    return pl.BlockSpec((None,) + tuple(shape), lambda i: (layer,) + (0,) * nd,
                        pipeline_mode=pl.Buffered(1))


def _shared_spec(shape):
    nd = len(shape)
    return pl.BlockSpec(tuple(shape), lambda i: (0,) * nd, pipeline_mode=pl.Buffered(1))


def _ffn(x, norm, w_in, w_out, layer):
    rows = x.shape[0]
    tm = min(_FFN_TM, rows)
    return pl.pallas_call(
        _ffn_kernel,
        grid=(rows // tm,),
        in_specs=[
            pl.BlockSpec((tm, D_MODEL), lambda i: (i, 0)),
            _const_spec((1, D_MODEL), layer),
            _const_spec((D_MODEL, 2 * D_FF), layer),
            _const_spec((D_FF, D_MODEL), layer),
        ],
        out_specs=pl.BlockSpec((tm, D_MODEL), lambda i: (i, 0)),
        out_shape=jax.ShapeDtypeStruct((rows, D_MODEL), F32),
        compiler_params=pltpu.CompilerParams(
            dimension_semantics=("arbitrary",), vmem_limit_bytes=_VMEM_LIMIT_BYTES),
        name="ffn",
    )(x, norm, w_in, w_out)


def _band_attention(i, qbuf, kbuf, vbuf, bias_ref, obuf, s_bufs, p_bufs):
    n_items = (TM // QH) * N_HEADS
    first_valid = jnp.where(i > 0, 0, TM)
    col = lax.broadcasted_iota(jnp.int32, (QB, KB), 1)

    def item(n):
        return n % N_HEADS, (n // N_HEADS) * QH

    def scores(n, s_ref):
        hh, r0 = item(n)
        s_ref[...] = _dot_nt(qbuf[hh, pl.ds(r0, QH), :], kbuf[hh, pl.ds(r0, KH), :])

    def softmax(n, s_ref, p_ref):
        hh, r0 = item(n)
        for a in range(QH // QB):
            sa = s_ref[a * QB:(a + 1) * QB, a * QB:a * QB + KB] + bias_ref[hh]
            sa = jnp.where(col >= first_valid - r0 - a * QB, sa, NEG_INF)
            p_ref[a * QB:(a + 1) * QB, a * QB:a * QB + KB] = jnp.exp2(
                sa - jnp.max(sa, axis=-1, keepdims=True)).astype(BF16)

    def values(n, p_ref):
        hh, r0 = item(n)
        o = _dot(p_ref[...], vbuf[hh, pl.ds(r0, KH), :])
        obuf[hh, pl.ds(r0, QH), :] = o[:, :HEAD_DIM] / o[:, HEAD_DIM:HEAD_DIM + 1]

    s0, s1 = s_bufs
    p0, p1 = p_bufs
    scores(0, s0)
    softmax(0, s0, p0)
    scores(1, s1)

    def group(k, s_cur, p_cur, s_nxt, p_prev):
        values(k - 1, p_prev)
        softmax(k, s_cur, p_cur)
        scores(k + 1, s_nxt)

    for k in range(1, n_items - 1):
        if k % 2:
            group(k, s1, p1, s0, p0)
        else:
            group(k, s0, p0, s1, p1)
    values(n_items - 2, p0)
    softmax(n_items - 1, s1, p1)
    values(n_items - 1, p1)


def _mix_prompt_kernel(x_ref, g_ref, win_ref, gb_ref, qg_ref, kg_ref, ones_ref, bias_ref,
                       pw_ref, ps_ref, cw_ref, cbias_ref, wa_ref, wp_ref, wc_ref, wo_ref,
                       y_ref, ko_ref, vo_ref, po_ref, co_ref,
                       qbuf, kbuf, vbuf, obuf, s0buf, s1buf, p0buf, p1buf, pext, cext):
    i = pl.program_id(0)

    @pl.when(i == 0)
    def _():
        kbuf[:, 0:TM, :] = jnp.zeros((N_HEADS, TM, HEAD_DIM), BF16)
        ones_col = lax.broadcasted_iota(jnp.int32, (N_HEADS, 2 * TM, VW), 2) == HEAD_DIM
        vbuf[...] = jnp.where(ones_col, 1.0, 0.0).astype(BF16)
        pext[0:_POOL_PAD, :] = jnp.zeros((_POOL_PAD, POOL_WIDTH), F32)
        cext[0:_CONV_PAD, :] = jnp.zeros((_CONV_PAD, CONV_WIDTH), F32)
        p0buf[...] = jnp.zeros((QH, KH), BF16)
        p1buf[...] = jnp.zeros((QH, KH), BF16)

    x = x_ref[...]
    h = _rms(x, g_ref[...]).astype(BF16)

    qkv = _dot(h, win_ref[:, _Q0:_P0])
    q = qkv[:, _Q0:_K0]
    k = qkv[:, _K0:_V0]
    v = qkv[:, _V0:_P0]
    qn, kn = _head_rms2(q, k, qg_ref[...], kg_ref[...], ones_ref)
    qn = qn * (ATTN_SCALE * LOG2E)
    ko_ref[...] = kn
    vo_ref[...] = v
    for hh in range(N_HEADS):
        sl = slice(hh * HEAD_DIM, (hh + 1) * HEAD_DIM)
        qbuf[hh] = qn[:, sl].astype(BF16)
        kbuf[hh, TM:2 * TM, :] = kn[:, sl].astype(BF16)
        vbuf[hh, TM:2 * TM, 0:HEAD_DIM] = v[:, sl].astype(BF16)

    side_in = _dot(h, win_ref[:, _P0:_G0])
    pu = side_in[:, 0:POOL_WIDTH]
    pext[_POOL_PAD:_POOL_PAD + TM, :] = pu
    d = _pool_delta(pext, pu, TM, i * TM)
    o_pool = _dot(d.astype(BF16), pw_ref[...]) * ps_ref[...]
    po_ref[...] = pext[_POOL_PAD + TM - POOL_HIST:_POOL_PAD + TM, :]
    pext[0:_POOL_PAD, :] = pext[TM:TM + _POOL_PAD, :]

    cu = side_in[:, _CU0 - _P0:_CB0 - _P0]
    cgate = side_in[:, _CB0 - _P0:_CC0 - _P0]
    cc = side_in[:, _CC0 - _P0:_G0 - _P0]
    cext[_CONV_PAD:_CONV_PAD + TM, :] = cc * cu
    o_conv = cgate * _conv_taps(cext, TM, cw_ref, cbias_ref)
    co_ref[...] = cext[_CONV_PAD + TM - (CONV_K - 1):_CONV_PAD + TM, :]
    cext[0:_CONV_PAD, :] = cext[TM:TM + _CONV_PAD, :]

    side = _side_branches(h, o_pool, o_conv, win_ref, gb_ref, wp_ref, wc_ref)
    attn_gate = _gate(h, 0, win_ref, gb_ref)

    _band_attention(i, qbuf, kbuf, vbuf, bias_ref, obuf, (s0buf, s1buf), (p0buf, p1buf))

    o_attn = jnp.concatenate([obuf[hh] for hh in range(N_HEADS)], axis=1)
    y_ref[...] = _project_out(x, attn_gate, o_attn, side, wa_ref, wo_ref)

    kbuf[:, 0:TM, :] = kbuf[:, TM:2 * TM, :]
    vbuf[:, 0:TM, :] = vbuf[:, TM:2 * TM, :]


def _mix_prompt(x, layer, p):
    rows = x.shape[0]
    in_w = p["mix_w_in"].shape[-1]
    return pl.pallas_call(
        _mix_prompt_kernel,
        grid=(rows // TM,),
        in_specs=[
            pl.BlockSpec((TM, D_MODEL), lambda i: (i, 0)),
            _const_spec((1, D_MODEL), layer),
            _const_spec((D_MODEL, in_w), layer),
            _const_spec((1, 3 * D_MODEL), layer),
            _const_spec((1, ATTN_WIDTH), layer),
            _const_spec((1, ATTN_WIDTH), layer),
            _shared_spec((ATTN_WIDTH, ATTN_WIDTH)),
            _const_spec((N_HEADS, QB, KB), layer),
            _const_spec((POOL_WIDTH, POOL_WIDTH), layer),
            _const_spec((1, POOL_WIDTH), layer),
            _const_spec((CONV_K, CONV_WIDTH), layer),
            _const_spec((1, CONV_WIDTH), layer),
            _const_spec((ATTN_WIDTH, D_MODEL), layer),
            _const_spec((POOL_WIDTH, D_MODEL), layer),
            _const_spec((CONV_WIDTH, D_MODEL), layer),
            _const_spec((D_MODEL, D_MODEL), layer),
        ],
        out_specs=[
            pl.BlockSpec((TM, D_MODEL), lambda i: (i, 0)),
            pl.BlockSpec((TM, ATTN_WIDTH), lambda i: (0, 0)),
            pl.BlockSpec((TM, ATTN_WIDTH), lambda i: (0, 0)),
            pl.BlockSpec((POOL_HIST, POOL_WIDTH), lambda i: (0, 0)),
            pl.BlockSpec((CONV_K - 1, CONV_WIDTH), lambda i: (0, 0)),
        ],
        out_shape=[
            jax.ShapeDtypeStruct((rows, D_MODEL), F32),
            jax.ShapeDtypeStruct((TM, ATTN_WIDTH), F32),
            jax.ShapeDtypeStruct((TM, ATTN_WIDTH), F32),
            jax.ShapeDtypeStruct((POOL_HIST, POOL_WIDTH), F32),
            jax.ShapeDtypeStruct((CONV_K - 1, CONV_WIDTH), F32),
        ],
        scratch_shapes=[
            pltpu.VMEM((N_HEADS, TM, HEAD_DIM), BF16),
            pltpu.VMEM((N_HEADS, 2 * TM, HEAD_DIM), BF16),
            pltpu.VMEM((N_HEADS, 2 * TM, VW), BF16),
            pltpu.VMEM((N_HEADS, TM, HEAD_DIM), F32),
            pltpu.VMEM((QH, KH), F32),
            pltpu.VMEM((QH, KH), F32),
            pltpu.VMEM((QH, KH), BF16),
            pltpu.VMEM((QH, KH), BF16),
            pltpu.VMEM((_POOL_PAD + TM, POOL_WIDTH), F32),
            pltpu.VMEM((_CONV_PAD + TM, CONV_WIDTH), F32),
        ],
        compiler_params=pltpu.CompilerParams(
            dimension_semantics=("arbitrary",), vmem_limit_bytes=_VMEM_LIMIT_BYTES),
        name="mix_prompt",
    )(x, p["mix_norm"], p["mix_w_in"], p["gate_b"], p["q_norm"], p["k_norm"], p["head_ones"],
      p["bias_prompt"], p["pool_w"], p["pool_scale"], p["conv_w"], p["conv_b"],
      p["w_branch_attn"], p["w_branch_pool"], p["w_branch_conv"], p["w_out"])


def _mix_sample_kernel(t_new, x_ref, g_ref, win_ref, gb_ref, qg_ref, kg_ref, ones_ref,
                       b1_ref, b2_ref, hm_ref, kct_ref, vct_ref, sp_ref, sc_ref,
                       pw_ref, ps_ref, cw_ref, cbias_ref, wa_ref, wp_ref, wc_ref, wo_ref,
                       y_ref, ko_ref, vo_ref, po_ref, co_ref,
                       hbuf, qbuf, knbuf, vnbuf, obuf, pubuf, dbuf, cinbuf, ybuf, pext, cext):
    b = pl.program_id(0)
    T = t_new

    @pl.when(b == 0)
    def _():
        h = _rms(x_ref[...], g_ref[...]).astype(BF16)
        hbuf[...] = h
        q = _dot(h, win_ref[:, _Q0:_Q0 + ATTN_WIDTH])
        k = _dot(h, win_ref[:, _K0:_K0 + ATTN_WIDTH])
        v = _dot(h, win_ref[:, _V0:_V0 + ATTN_WIDTH])
        qn, kn = _head_rms2(q, k, qg_ref[...], kg_ref[...], ones_ref)
        qn = qn * ATTN_SCALE
        ko_ref[...] = kn
        vo_ref[...] = v
        qbuf[...] = qn.astype(BF16)
        knbuf[...] = kn.astype(BF16)
        vnbuf[...] = v.astype(BF16)
        pubuf[...] = _dot(h, win_ref[:, _P0:_P0 + POOL_WIDTH])
        cu = _dot(h, win_ref[:, _CU0:_CU0 + CONV_WIDTH])
        cc = _dot(h, win_ref[:, _CC0:_CC0 + CONV_WIDTH])
        cinbuf[...] = cc * cu

    for s in range(kct_ref.shape[0]):
        stream = b * kct_ref.shape[0] + s
        r0 = pl.multiple_of(stream * T, T)
        pext_s, cext_s = pext.at[s], cext.at[s]

        q_b = qbuf[pl.ds(r0, T), :]
        hm = hm_ref[...]
        qexp = jnp.where(hm > 0, jnp.concatenate([q_b] * N_HEADS, axis=0), jnp.zeros((), BF16))
        kct = kct_ref[s].astype(BF16)
        vct = vct_ref[s].astype(BF16)
        s1 = _dot(qexp, kct) + b1_ref[...]
        s2 = _dot_nt(qexp, knbuf[pl.ds(r0, T), :]) + b2_ref[...]
        m = jnp.maximum(jnp.max(s1, axis=-1, keepdims=True), jnp.max(s2, axis=-1, keepdims=True))
        p1 = jnp.exp(s1 - m)
        p2 = jnp.exp(s2 - m)
        denom = jnp.sum(p1, axis=-1, keepdims=True) + jnp.sum(p2, axis=-1, keepdims=True)
        oall = (_dot_nt(p1.astype(BF16), vct) + _dot(p2.astype(BF16), vnbuf[pl.ds(r0, T), :])) / denom
        oall = oall * hm
        o = oall[0:T, :]
        for hh in range(1, N_HEADS):
            o = o + oall[hh * T:(hh + 1) * T, :]
        obuf[pl.ds(r0, T), :] = o

        pu = pubuf[pl.ds(r0, T), :]
        pext_s[_POOL_PAD - POOL_HIST:_POOL_PAD, :] = sp_ref[stream]
        pext_s[_POOL_PAD:_POOL_PAD + T, :] = pu
        dbuf[pl.ds(r0, T), :] = _pool_delta(pext_s, pu, T, PAST_LEN)
        po_ref[stream] = pext_s[_POOL_PAD + T - POOL_HIST:_POOL_PAD + T, :]

        cext_s[_CONV_PAD - (CONV_K - 1):_CONV_PAD, :] = sc_ref[stream]
        cext_s[_CONV_PAD:_CONV_PAD + T, :] = cinbuf[pl.ds(r0, T), :]
        ybuf[pl.ds(r0, T), :] = _conv_taps(cext_s, T, cw_ref, cbias_ref)
        co_ref[stream] = cext_s[_CONV_PAD + T - (CONV_K - 1):_CONV_PAD + T, :]

    @pl.when(b == pl.num_programs(0) - 1)
    def _():
        h = hbuf[...]
        o_pool = _dot(dbuf[...].astype(BF16), pw_ref[...]) * ps_ref[...]
        o_conv = _dot(h, win_ref[:, _CB0:_CB0 + CONV_WIDTH]) * ybuf[...]
        side = _side_branches(h, o_pool, o_conv, win_ref, gb_ref, wp_ref, wc_ref)
        y_ref[...] = _project_out(x_ref[...], _gate(h, 0, win_ref, gb_ref), obuf[...], side,
                                  wa_ref, wo_ref)


def _mix_sample(x, cache_kt, cache_vt, state_pool, state_conv, layer, p):
    rows = x.shape[0]
    n_streams, r_cache = cache_kt.shape[1], cache_kt.shape[3]
    t_new = rows // n_streams
    in_w = p["mix_w_in"].shape[-1]
    cache_spec = pl.BlockSpec((None, _SAMPLE_STREAMS_PER_STEP, ATTN_WIDTH, r_cache),
                              lambda b: (layer, b, 0, 0))
    full = lambda shape: pl.BlockSpec(tuple(shape), lambda b: (0,) * len(shape))
    return pl.pallas_call(
        functools.partial(_mix_sample_kernel, t_new),
        grid=(n_streams // _SAMPLE_STREAMS_PER_STEP,),
        in_specs=[
            _shared_spec((rows, D_MODEL)),
            _const_spec((1, D_MODEL), layer),
            _const_spec((D_MODEL, in_w), layer),
            _const_spec((1, 3 * D_MODEL), layer),
            _const_spec((1, ATTN_WIDTH), layer),
            _const_spec((1, ATTN_WIDTH), layer),
            _shared_spec((ATTN_WIDTH, ATTN_WIDTH)),
            _const_spec((N_HEADS * t_new, r_cache), layer),
            _const_spec((N_HEADS * t_new, t_new), layer),
            _shared_spec((N_HEADS * t_new, ATTN_WIDTH)),
            cache_spec,
            cache_spec,
            _const_spec((n_streams, POOL_HIST, POOL_WIDTH), layer),
            _const_spec((n_streams, CONV_K - 1, CONV_WIDTH), layer),
            _const_spec((POOL_WIDTH, POOL_WIDTH), layer),
            _const_spec((1, POOL_WIDTH), layer),
            _const_spec((CONV_K, CONV_WIDTH), layer),
            _const_spec((1, CONV_WIDTH), layer),
            _const_spec((ATTN_WIDTH, D_MODEL), layer),
            _const_spec((POOL_WIDTH, D_MODEL), layer),
            _const_spec((CONV_WIDTH, D_MODEL), layer),
            _const_spec((D_MODEL, D_MODEL), layer),
        ],
        out_specs=[
            full((rows, D_MODEL)),
            full((rows, ATTN_WIDTH)),
            full((rows, ATTN_WIDTH)),
            full((n_streams, POOL_HIST, POOL_WIDTH)),
            full((n_streams, CONV_K - 1, CONV_WIDTH)),
        ],
        out_shape=[
            jax.ShapeDtypeStruct((rows, D_MODEL), F32),
            jax.ShapeDtypeStruct((rows, ATTN_WIDTH), F32),
            jax.ShapeDtypeStruct((rows, ATTN_WIDTH), F32),
            jax.ShapeDtypeStruct((n_streams, POOL_HIST, POOL_WIDTH), F32),
            jax.ShapeDtypeStruct((n_streams, CONV_K - 1, CONV_WIDTH), F32),
        ],
        scratch_shapes=[
            pltpu.VMEM((rows, D_MODEL), BF16),
            pltpu.VMEM((rows, ATTN_WIDTH), BF16),
            pltpu.VMEM((rows, ATTN_WIDTH), BF16),
            pltpu.VMEM((rows, ATTN_WIDTH), BF16),
            pltpu.VMEM((rows, ATTN_WIDTH), F32),
            pltpu.VMEM((rows, POOL_WIDTH), F32),
            pltpu.VMEM((rows, POOL_WIDTH), F32),
            pltpu.VMEM((rows, CONV_WIDTH), F32),
            pltpu.VMEM((rows, CONV_WIDTH), F32),
            pltpu.VMEM((_SAMPLE_STREAMS_PER_STEP, _POOL_PAD + t_new, POOL_WIDTH), F32),
            pltpu.VMEM((_SAMPLE_STREAMS_PER_STEP, _CONV_PAD + t_new, CONV_WIDTH), F32),
        ],
        compiler_params=pltpu.CompilerParams(
            dimension_semantics=("arbitrary",), vmem_limit_bytes=_VMEM_LIMIT_BYTES),
        name="mix_sample",
    )(x, p["mix_norm"], p["mix_w_in"], p["gate_b"], p["q_norm"], p["k_norm"], p["head_ones"],
      p["bias_cache"], p["bias_new"], p["head_mask"], cache_kt, cache_vt, state_pool, state_conv,
      p["pool_w"], p["pool_scale"], p["conv_w"], p["conv_b"],
      p["w_branch_attn"], p["w_branch_pool"], p["w_branch_conv"], p["w_out"])


def _rel_lookup(rel_bias, lo, hi):
    lead = rel_bias.shape[:-1]
    n_left = max(0, min(hi, -REL_CLIP) - lo)
    n_right = max(0, hi - max(lo, REL_CLIP + 1))
    a, b = max(lo, -REL_CLIP), min(hi, REL_CLIP + 1)
    parts = []
    if n_left:
        parts.append(jnp.broadcast_to(rel_bias[..., :1], lead + (n_left,)))
    if b > a:
        parts.append(rel_bias[..., a + REL_CLIP:b + REL_CLIP])
    if n_right:
        parts.append(jnp.broadcast_to(rel_bias[..., -1:], lead + (n_right,)))
    return jnp.concatenate(parts, axis=-1)


def _toeplitz(g, rows):
    n = g.shape[-1]
    lead = g.shape[:-1]
    t = jnp.broadcast_to(g[..., None, :], lead + (rows, n)).reshape(lead + (rows * n,))
    return t[..., :rows * (n - 1)].reshape(lead + (rows, n - 1))


def _band_bias(rel_bias, n_query, n_past, n_key):
    g = jnp.concatenate([_rel_lookup(rel_bias, -n_past, n_key - n_past),
                         _rel_lookup(rel_bias, -n_past - n_query, -n_past)], axis=-1)
    return _toeplitz(g, n_query)[..., :n_key].astype(F32)


def _prompt_bias(rel_bias):
    band = _band_bias(rel_bias, CHUNK, N_PREV_CHUNKS * CHUNK, BAND) * LOG2E
    masked = jnp.full(band.shape[:-1] + (KB - BAND,), NEG_INF, F32)
    return jnp.concatenate([jnp.concatenate([band, masked], axis=-1),
                            jnp.concatenate([masked, band], axis=-1)], axis=-2)


def _sample_bias(rel_bias, t_new, r_cache):
    depth = rel_bias.shape[0]
    table = _band_bias(rel_bias, t_new, r_cache, r_cache + t_new)
    table = table.reshape(depth, N_HEADS * t_new, r_cache + t_new)
    return table[..., :r_cache], table[..., r_cache:]


def kernel(x_prompt, x_sample, cache_attn_k, cache_attn_v, state_pool, state_conv, ffn1_norm, ffn1_w_in, ffn1_w_out, mix_norm, mix_w_in, gate_b, q_norm, k_norm, rel_bias, pool_w, pool_scale, conv_w, conv_b, w_branch_attn, w_branch_pool, w_branch_conv, w_out, ffn2_norm, ffn2_w_in, ffn2_w_out):
    batch, seq, _ = x_prompt.shape
    n_streams, t_new, _ = x_sample.shape
    depth = mix_w_in.shape[0]
    r_cache = cache_attn_k.shape[2]
    assert batch == 1 and seq % TM == 0 and seq >= TM
    assert r_cache == N_PREV_CHUNKS * CHUNK and PAST_LEN % CHUNK == 0 and t_new <= CHUNK
    assert n_streams % _SAMPLE_STREAMS_PER_STEP == 0

    row = lambda a: a.reshape(depth, 1, a.shape[-1])
    head = np.arange(ATTN_WIDTH) // HEAD_DIM
    n_groups = len(POOL_WINDOWS)
    pool_bd = jnp.einsum("lgcd,gh->lgchd", pool_w, jnp.eye(n_groups, dtype=pool_w.dtype))
    bias_cache, bias_new = _sample_bias(rel_bias, t_new, r_cache)
    p = dict(
        mix_norm=row(mix_norm), mix_w_in=mix_w_in.astype(BF16), gate_b=row(gate_b),
        q_norm=row(jnp.tile(q_norm, (1, N_HEADS))), k_norm=row(jnp.tile(k_norm, (1, N_HEADS))),
        head_ones=jnp.asarray(head[:, None] == head[None, :], BF16),
        bias_prompt=_prompt_bias(rel_bias), bias_cache=bias_cache, bias_new=bias_new,
        head_mask=jnp.asarray(np.repeat(np.arange(N_HEADS), t_new)[:, None] == head[None, :], F32),
        pool_w=pool_bd.reshape(depth, POOL_WIDTH, POOL_WIDTH).astype(BF16),
        pool_scale=row(pool_scale), conv_w=conv_w, conv_b=row(conv_b),
        w_branch_attn=w_branch_attn.astype(BF16), w_branch_pool=w_branch_pool.astype(BF16),
        w_branch_conv=w_branch_conv.astype(BF16), w_out=w_out.astype(BF16),
    )
    f1 = (row(ffn1_norm), ffn1_w_in.astype(BF16), ffn1_w_out.astype(BF16))
    f2 = (row(ffn2_norm), ffn2_w_in.astype(BF16), ffn2_w_out.astype(BF16))

    cache_kt = cache_attn_k.transpose(0, 1, 3, 4, 2).reshape(depth, n_streams, ATTN_WIDTH, r_cache)
    cache_vt = cache_attn_v.transpose(0, 1, 3, 4, 2).reshape(depth, n_streams, ATTN_WIDTH, r_cache)

    xp = x_prompt.reshape(seq, D_MODEL)
    xs = x_sample.reshape(n_streams * t_new, D_MODEL)
    outs = [[] for _ in range(8)]
    for l in range(depth):
        xp = _ffn(xp, *f1, l)
        xs = _ffn(xs, *f1, l)
        xp, kp, vp, pp, cp = _mix_prompt(xp, l, p)
        xs, ks, vs, ps, cs = _mix_sample(xs, cache_kt, cache_vt, state_pool, state_conv, l, p)
        xp = _ffn(xp, *f2, l)
        xs = _ffn(xs, *f2, l)
        for acc, val in zip(outs, (kp, vp, pp, cp, ks, vs, ps, cs)):
            acc.append(val)
    kp, vp, pp, cp, ks, vs, ps, cs = (jnp.stack(o) for o in outs)
    r_keep = min(N_PREV_CHUNKS * CHUNK, seq)
    return (xp.reshape(batch, seq, D_MODEL), xs.reshape(n_streams, t_new, D_MODEL),
            kp.reshape(depth, batch, r_keep, N_HEADS, HEAD_DIM),
            vp.reshape(depth, batch, r_keep, N_HEADS, HEAD_DIM),
            pp.reshape(depth, batch, POOL_HIST, POOL_WIDTH),
            cp.reshape(depth, batch, CONV_K - 1, CONV_WIDTH),
            ks.reshape(depth, n_streams, t_new, N_HEADS, HEAD_DIM),
            vs.reshape(depth, n_streams, t_new, N_HEADS, HEAD_DIM),
            ps, cs)
```

```python
import functools

import jax
import jax.numpy as jnp
import numpy as np
from jax import lax
from jax.experimental import pallas as pl
from jax.experimental.pallas import tpu as pltpu

D_MODEL = 1024
DEPTH = 4
PAST_LEN = 1024
CHUNK = 64
N_PREV_CHUNKS = 8
BAND = (N_PREV_CHUNKS + 1) * CHUNK
N_HEADS = 8
HEAD_DIM = 64
ATTN_WIDTH = N_HEADS * HEAD_DIM
ATTN_SCALE = HEAD_DIM ** -0.5
REL_CLIP = 128
POOL_WINDOWS = (2, 4, 8, 16)
POOL_GROUP = 64
POOL_WIDTH = len(POOL_WINDOWS) * POOL_GROUP
POOL_HIST = max(POOL_WINDOWS) - 1
CONV_WIDTH = 256
CONV_K = 3
D_FF = 2816
EPS = 1e-6
NEG_INF = -1e30

_Q0, _K0, _V0 = 0, ATTN_WIDTH, 2 * ATTN_WIDTH
_P0 = 3 * ATTN_WIDTH
_CU0 = _P0 + POOL_WIDTH
_CB0 = _CU0 + CONV_WIDTH
_CC0 = _CB0 + CONV_WIDTH
_G0 = _CC0 + CONV_WIDTH

TM = N_PREV_CHUNKS * CHUNK
QB = 2 * CHUNK
KB = BAND + CHUNK
QH = 2 * QB
KH = KB + QB
VW = 2 * HEAD_DIM
LOG2E = 1.4426950408889634
_FF_CHUNKS = ((0, 1024), (1024, 2048), (2048, D_FF))
_FFN_TM = 2 * TM
_FF_SLICE = D_FF // 2
_POOL_PAD = 16
_CONV_PAD = 8
_SAMPLE_STREAMS_PER_STEP = 4

_V7X_VMEM_BYTES = 64 * 2 ** 20
_VMEM_UNSCOPED_BYTES = 8 * 2 ** 20
_VMEM_LIMIT_BYTES = _V7X_VMEM_BYTES - _VMEM_UNSCOPED_BYTES

F32 = jnp.float32
BF16 = jnp.bfloat16


def _dot(a, b):
    return jnp.dot(a, b, preferred_element_type=F32)


def _dot_nt(a, b):
    return lax.dot_general(a, b, (((1,), (1,)), ((), ())), preferred_element_type=F32)


def _rms(x, g):
    ms = jnp.mean(x * x, axis=-1, keepdims=True)
    return x * lax.rsqrt(ms + EPS) * g


def _head_rms2(q, k, qg, kg, head_ones_ref):
    rows = q.shape[0]
    sq = jnp.concatenate([(q * q).astype(BF16), (k * k).astype(BF16)], axis=0)
    ss = _dot(sq, head_ones_ref[...]) * (1.0 / HEAD_DIM)
    return (q * lax.rsqrt(ss[:rows] + EPS) * qg, k * lax.rsqrt(ss[rows:] + EPS) * kg)


def _pool_delta(ext_ref, pu, rows, pos0):
    def back(j):
        return ext_ref[_POOL_PAD - j:_POOL_PAD - j + rows, :]
    s2 = pu + back(1)
    s4 = s2 + back(2) + back(3)
    s8 = s4 + back(4) + back(5) + back(6) + back(7)
    s16 = s8
    for j in range(8, 16):
        s16 = s16 + back(j)
    lane = lax.broadcasted_iota(jnp.int32, (rows, POOL_WIDTH), 1)
    pos = lax.broadcasted_iota(jnp.int32, (rows, POOL_WIDTH), 0) + pos0
    g0, g1, g2 = lane < POOL_GROUP, lane < 2 * POOL_GROUP, lane < 3 * POOL_GROUP
    wsum = jnp.where(g0, s2, jnp.where(g1, s4, jnp.where(g2, s8, s16)))
    wlen = jnp.where(g0, 2, jnp.where(g1, 4, jnp.where(g2, 8, 16)))
    cnt = jnp.minimum(wlen, pos + 1).astype(F32)
    return wsum / cnt - pu


def _conv_taps(ext_ref, rows, cw_ref, cbias_ref):
    y = cbias_ref[...]
    for t in range(CONV_K):
        off = _CONV_PAD - (CONV_K - 1) + t
        y = y + ext_ref[off:off + rows, :] * cw_ref[t:t + 1, :]
    return y


def _gate(h, idx, win_ref, gb_ref):
    lo, hi = _G0 + idx * D_MODEL, _G0 + (idx + 1) * D_MODEL
    return jax.nn.sigmoid(_dot(h, win_ref[:, lo:hi]) + gb_ref[:, idx * D_MODEL:(idx + 1) * D_MODEL])


def _side_branches(h, o_pool, o_conv, win_ref, gb_ref, wp_ref, wc_ref):
    return (_gate(h, 1, win_ref, gb_ref) * _dot(o_pool.astype(BF16), wp_ref[...])
            + _gate(h, 2, win_ref, gb_ref) * _dot(o_conv.astype(BF16), wc_ref[...]))


def _project_out(x, attn_gate, o_attn, side, wa_ref, wo_ref):
    m = attn_gate * _dot(o_attn.astype(BF16), wa_ref[...]) + side
    return x + _dot(m.astype(BF16), wo_ref[...])


def _ffn_kernel(x_ref, g_ref, win_ref, wout_ref, o_ref):
    x = x_ref[...]
    h = _rms(x, g_ref[...]).astype(BF16)
    acc = None
    for a, b in _FF_CHUNKS:
        gate = _dot(h, win_ref[:, a:b])
        up = _dot(h, win_ref[:, D_FF + a:D_FF + b])
        act = (gate * jax.nn.sigmoid(gate) * up).astype(BF16)
        part = _dot(act, wout_ref[a:b, :])
        acc = part if acc is None else acc + part
    o_ref[...] = x + 0.5 * acc


def _const_spec(shape, layer):
    nd = len(shape)
    return pl.BlockSpec((None,) + tuple(shape), lambda i: (layer,) + (0,) * nd,
                        pipeline_mode=pl.Buffered(1))


def _shared_spec(shape):
    nd = len(shape)
    return pl.BlockSpec(tuple(shape), lambda i: (0,) * nd, pipeline_mode=pl.Buffered(1))


def _ffn_streamed_kernel(x_ref, g_ref, wg_ref, wu_ref, wd_ref, o_ref, hbuf):
    c = pl.program_id(0)

    @pl.when(c == 0)
    def _():
        x = x_ref[...]
        hbuf[...] = _rms(x, g_ref[...]).astype(BF16)
        o_ref[...] = x

    h = hbuf[...]
    gate = _dot(h, wg_ref[...])
    up = _dot(h, wu_ref[...])
    act = (gate * jax.nn.sigmoid(gate) * up).astype(BF16)
    o_ref[...] += 0.5 * _dot(act, wd_ref[...])


def _ffn_streamed(x, norm, w_in, w_out, layer):
    rows = x.shape[0]
    n_slices = D_FF // _FF_SLICE
    whole = lambda shape: pl.BlockSpec(tuple(shape), lambda c: (0,) * len(shape))
    return pl.pallas_call(
        _ffn_streamed_kernel,
        grid=(n_slices,),
        in_specs=[
            whole((rows, D_MODEL)),
            pl.BlockSpec((None, 1, D_MODEL), lambda c: (layer, 0, 0)),
            pl.BlockSpec((None, D_MODEL, _FF_SLICE), lambda c: (layer, 0, c)),
            pl.BlockSpec((None, D_MODEL, _FF_SLICE), lambda c: (layer, 0, n_slices + c)),
            pl.BlockSpec((None, _FF_SLICE, D_MODEL), lambda c: (layer, c, 0)),
        ],
        out_specs=whole((rows, D_MODEL)),
        out_shape=jax.ShapeDtypeStruct((rows, D_MODEL), F32),
        scratch_shapes=[pltpu.VMEM((rows, D_MODEL), BF16)],
        compiler_params=pltpu.CompilerParams(
            dimension_semantics=("arbitrary",), vmem_limit_bytes=_VMEM_LIMIT_BYTES),
        name="ffn_streamed",
    )(x, norm, w_in, w_in, w_out)


def _ffn(x, norm, w_in, w_out, layer):
    rows = x.shape[0]
    if rows <= TM:
        return _ffn_streamed(x, norm, w_in, w_out, layer)
    tm = min(_FFN_TM, rows)
    return pl.pallas_call(
        _ffn_kernel,
        grid=(rows // tm,),
        in_specs=[
            pl.BlockSpec((tm, D_MODEL), lambda i: (i, 0)),
            _const_spec((1, D_MODEL), layer),
            _const_spec((D_MODEL, 2 * D_FF), layer),
            _const_spec((D_FF, D_MODEL), layer),
        ],
        out_specs=pl.BlockSpec((tm, D_MODEL), lambda i: (i, 0)),
        out_shape=jax.ShapeDtypeStruct((rows, D_MODEL), F32),
        compiler_params=pltpu.CompilerParams(
            dimension_semantics=("arbitrary",), vmem_limit_bytes=_VMEM_LIMIT_BYTES),
        name="ffn",
    )(x, norm, w_in, w_out)


def _band_attention(i, qbuf, kbuf, vbuf, bias_ref, obuf, s_bufs, p_bufs):
    n_items = (TM // QH) * N_HEADS
    first_valid = jnp.where(i > 0, 0, TM)
    col = lax.broadcasted_iota(jnp.int32, (QB, KB), 1)

    def item(n):
        return n % N_HEADS, (n // N_HEADS) * QH

    def scores(n, s_ref):
        hh, r0 = item(n)
        s_ref[...] = _dot_nt(qbuf[hh, pl.ds(r0, QH), :], kbuf[hh, pl.ds(r0, KH), :])

    def softmax(n, s_ref, p_ref):
        hh, r0 = item(n)
        for a in range(QH // QB):
            sa = s_ref[a * QB:(a + 1) * QB, a * QB:a * QB + KB] + bias_ref[hh]
            sa = jnp.where(col >= first_valid - r0 - a * QB, sa, NEG_INF)
            p_ref[a * QB:(a + 1) * QB, a * QB:a * QB + KB] = jnp.exp2(
                sa - jnp.max(sa, axis=-1, keepdims=True)).astype(BF16)

    def values(n, p_ref):
        hh, r0 = item(n)
        o = _dot(p_ref[...], vbuf[hh, pl.ds(r0, KH), :])
        obuf[hh, pl.ds(r0, QH), :] = o[:, :HEAD_DIM] / o[:, HEAD_DIM:HEAD_DIM + 1]

    s0, s1 = s_bufs
    p0, p1 = p_bufs
    scores(0, s0)
    softmax(0, s0, p0)
    scores(1, s1)

    def group(k, s_cur, p_cur, s_nxt, p_prev):
        values(k - 1, p_prev)
        softmax(k, s_cur, p_cur)
        scores(k + 1, s_nxt)

    for k in range(1, n_items - 1):
        if k % 2:
            group(k, s1, p1, s0, p0)
        else:
            group(k, s0, p0, s1, p1)
    values(n_items - 2, p0)
    softmax(n_items - 1, s1, p1)
    values(n_items - 1, p1)


def _mix_prompt_kernel(x_ref, g_ref, win_ref, gb_ref, qg_ref, kg_ref, ones_ref, bias_ref,
                       pw_ref, ps_ref, cw_ref, cbias_ref, wa_ref, wp_ref, wc_ref, wo_ref,
                       y_ref, ko_ref, vo_ref, po_ref, co_ref,
                       qbuf, kbuf, vbuf, obuf, s0buf, s1buf, p0buf, p1buf, pext, cext):
    i = pl.program_id(0)

    @pl.when(i == 0)
    def _():
        kbuf[:, 0:TM, :] = jnp.zeros((N_HEADS, TM, HEAD_DIM), BF16)
        ones_col = lax.broadcasted_iota(jnp.int32, (N_HEADS, 2 * TM, VW), 2) == HEAD_DIM
        vbuf[...] = jnp.where(ones_col, 1.0, 0.0).astype(BF16)
        pext[0:_POOL_PAD, :] = jnp.zeros((_POOL_PAD, POOL_WIDTH), F32)
        cext[0:_CONV_PAD, :] = jnp.zeros((_CONV_PAD, CONV_WIDTH), F32)
        p0buf[...] = jnp.zeros((QH, KH), BF16)
        p1buf[...] = jnp.zeros((QH, KH), BF16)

    x = x_ref[...]
    h = _rms(x, g_ref[...]).astype(BF16)

    qkv = _dot(h, win_ref[:, _Q0:_P0])
    q = qkv[:, _Q0:_K0]
    k = qkv[:, _K0:_V0]
    v = qkv[:, _V0:_P0]
    qn, kn = _head_rms2(q, k, qg_ref[...], kg_ref[...], ones_ref)
    qn = qn * (ATTN_SCALE * LOG2E)
    ko_ref[...] = kn
    vo_ref[...] = v
    for hh in range(N_HEADS):
        sl = slice(hh * HEAD_DIM, (hh + 1) * HEAD_DIM)
        qbuf[hh] = qn[:, sl].astype(BF16)
        kbuf[hh, TM:2 * TM, :] = kn[:, sl].astype(BF16)
        vbuf[hh, TM:2 * TM, 0:HEAD_DIM] = v[:, sl].astype(BF16)

    side_in = _dot(h, win_ref[:, _P0:_G0])
    pu = side_in[:, 0:POOL_WIDTH]
    pext[_POOL_PAD:_POOL_PAD + TM, :] = pu
    d = _pool_delta(pext, pu, TM, i * TM)
    o_pool = _dot(d.astype(BF16), pw_ref[...]) * ps_ref[...]
    po_ref[...] = pext[_POOL_PAD + TM - POOL_HIST:_POOL_PAD + TM, :]
    pext[0:_POOL_PAD, :] = pext[TM:TM + _POOL_PAD, :]

    cu = side_in[:, _CU0 - _P0:_CB0 - _P0]
    cgate = side_in[:, _CB0 - _P0:_CC0 - _P0]
    cc = side_in[:, _CC0 - _P0:_G0 - _P0]
    cext[_CONV_PAD:_CONV_PAD + TM, :] = cc * cu
    o_conv = cgate * _conv_taps(cext, TM, cw_ref, cbias_ref)
    co_ref[...] = cext[_CONV_PAD + TM - (CONV_K - 1):_CONV_PAD + TM, :]
    cext[0:_CONV_PAD, :] = cext[TM:TM + _CONV_PAD, :]

    side = _side_branches(h, o_pool, o_conv, win_ref, gb_ref, wp_ref, wc_ref)
    attn_gate = _gate(h, 0, win_ref, gb_ref)

    _band_attention(i, qbuf, kbuf, vbuf, bias_ref, obuf, (s0buf, s1buf), (p0buf, p1buf))

    o_attn = jnp.concatenate([obuf[hh] for hh in range(N_HEADS)], axis=1)
    y_ref[...] = _project_out(x, attn_gate, o_attn, side, wa_ref, wo_ref)

    kbuf[:, 0:TM, :] = kbuf[:, TM:2 * TM, :]
    vbuf[:, 0:TM, :] = vbuf[:, TM:2 * TM, :]


def _mix_prompt(x, layer, p):
    rows = x.shape[0]
    in_w = p["mix_w_in"].shape[-1]
    return pl.pallas_call(
        _mix_prompt_kernel,
        grid=(rows // TM,),
        in_specs=[
            pl.BlockSpec((TM, D_MODEL), lambda i: (i, 0)),
            _const_spec((1, D_MODEL), layer),
            _const_spec((D_MODEL, in_w), layer),
            _const_spec((1, 3 * D_MODEL), layer),
            _const_spec((1, ATTN_WIDTH), layer),
            _const_spec((1, ATTN_WIDTH), layer),
            _shared_spec((ATTN_WIDTH, ATTN_WIDTH)),
            _const_spec((N_HEADS, QB, KB), layer),
            _const_spec((POOL_WIDTH, POOL_WIDTH), layer),
            _const_spec((1, POOL_WIDTH), layer),
            _const_spec((CONV_K, CONV_WIDTH), layer),
            _const_spec((1, CONV_WIDTH), layer),
            _const_spec((ATTN_WIDTH, D_MODEL), layer),
            _const_spec((POOL_WIDTH, D_MODEL), layer),
            _const_spec((CONV_WIDTH, D_MODEL), layer),
            _const_spec((D_MODEL, D_MODEL), layer),
        ],
        out_specs=[
            pl.BlockSpec((TM, D_MODEL), lambda i: (i, 0)),
            pl.BlockSpec((TM, ATTN_WIDTH), lambda i: (0, 0)),
            pl.BlockSpec((TM, ATTN_WIDTH), lambda i: (0, 0)),
            pl.BlockSpec((POOL_HIST, POOL_WIDTH), lambda i: (0, 0)),
            pl.BlockSpec((CONV_K - 1, CONV_WIDTH), lambda i: (0, 0)),
        ],
        out_shape=[
            jax.ShapeDtypeStruct((rows, D_MODEL), F32),
            jax.ShapeDtypeStruct((TM, ATTN_WIDTH), F32),
            jax.ShapeDtypeStruct((TM, ATTN_WIDTH), F32),
            jax.ShapeDtypeStruct((POOL_HIST, POOL_WIDTH), F32),
            jax.ShapeDtypeStruct((CONV_K - 1, CONV_WIDTH), F32),
        ],
        scratch_shapes=[
            pltpu.VMEM((N_HEADS, TM, HEAD_DIM), BF16),
            pltpu.VMEM((N_HEADS, 2 * TM, HEAD_DIM), BF16),
            pltpu.VMEM((N_HEADS, 2 * TM, VW), BF16),
            pltpu.VMEM((N_HEADS, TM, HEAD_DIM), F32),
            pltpu.VMEM((QH, KH), F32),
            pltpu.VMEM((QH, KH), F32),
            pltpu.VMEM((QH, KH), BF16),
            pltpu.VMEM((QH, KH), BF16),
            pltpu.VMEM((_POOL_PAD + TM, POOL_WIDTH), F32),
            pltpu.VMEM((_CONV_PAD + TM, CONV_WIDTH), F32),
        ],
        compiler_params=pltpu.CompilerParams(
            dimension_semantics=("arbitrary",), vmem_limit_bytes=_VMEM_LIMIT_BYTES),
        name="mix_prompt",
    )(x, p["mix_norm"], p["mix_w_in"], p["gate_b"], p["q_norm"], p["k_norm"], p["head_ones"],
      p["bias_prompt"], p["pool_w"], p["pool_scale"], p["conv_w"], p["conv_b"],
      p["w_branch_attn"], p["w_branch_pool"], p["w_branch_conv"], p["w_out"])


def _mix_sample_kernel(t_new, x_ref, g_ref, win_ref, gb_ref, qg_ref, kg_ref, ones_ref,
                       b1_ref, b2_ref, hm_ref, kct_ref, vct_ref, sp_ref, sc_ref,
                       pw_ref, ps_ref, cw_ref, cbias_ref, wa_ref, wp_ref, wc_ref, wo_ref,
                       y_ref, ko_ref, vo_ref, po_ref, co_ref,
                       hbuf, qbuf, knbuf, vnbuf, obuf, pubuf, dbuf, cinbuf, ybuf, pext, cext):
    b = pl.program_id(0)
    T = t_new

    @pl.when(b == 0)
    def _():
        h = _rms(x_ref[...], g_ref[...]).astype(BF16)
        hbuf[...] = h
        q = _dot(h, win_ref[:, _Q0:_Q0 + ATTN_WIDTH])
        k = _dot(h, win_ref[:, _K0:_K0 + ATTN_WIDTH])
        v = _dot(h, win_ref[:, _V0:_V0 + ATTN_WIDTH])
        qn, kn = _head_rms2(q, k, qg_ref[...], kg_ref[...], ones_ref)
        qn = qn * ATTN_SCALE
        ko_ref[...] = kn
        vo_ref[...] = v
        qbuf[...] = qn.astype(BF16)
        knbuf[...] = kn.astype(BF16)
        vnbuf[...] = v.astype(BF16)
        pubuf[...] = _dot(h, win_ref[:, _P0:_P0 + POOL_WIDTH])
        cu = _dot(h, win_ref[:, _CU0:_CU0 + CONV_WIDTH])
        cc = _dot(h, win_ref[:, _CC0:_CC0 + CONV_WIDTH])
        cinbuf[...] = cc * cu

    for s in range(kct_ref.shape[0]):
        stream = b * kct_ref.shape[0] + s
        r0 = pl.multiple_of(stream * T, T)
        pext_s, cext_s = pext.at[s], cext.at[s]

        q_b = qbuf[pl.ds(r0, T), :]
        hm = hm_ref[...]
        qexp = jnp.where(hm > 0, jnp.concatenate([q_b] * N_HEADS, axis=0), jnp.zeros((), BF16))
        kct = kct_ref[s].astype(BF16)
        vct = vct_ref[s].astype(BF16)
        s1 = _dot(qexp, kct) + b1_ref[...]
        s2 = _dot_nt(qexp, knbuf[pl.ds(r0, T), :]) + b2_ref[...]
        m = jnp.maximum(jnp.max(s1, axis=-1, keepdims=True), jnp.max(s2, axis=-1, keepdims=True))
        p1 = jnp.exp(s1 - m)
        p2 = jnp.exp(s2 - m)
        denom = jnp.sum(p1, axis=-1, keepdims=True) + jnp.sum(p2, axis=-1, keepdims=True)
        oall = (_dot_nt(p1.astype(BF16), vct) + _dot(p2.astype(BF16), vnbuf[pl.ds(r0, T), :])) / denom
        oall = oall * hm
        o = oall[0:T, :]
        for hh in range(1, N_HEADS):
            o = o + oall[hh * T:(hh + 1) * T, :]
        obuf[pl.ds(r0, T), :] = o

        pu = pubuf[pl.ds(r0, T), :]
        pext_s[_POOL_PAD - POOL_HIST:_POOL_PAD, :] = sp_ref[stream]
        pext_s[_POOL_PAD:_POOL_PAD + T, :] = pu
        dbuf[pl.ds(r0, T), :] = _pool_delta(pext_s, pu, T, PAST_LEN)
        po_ref[stream] = pext_s[_POOL_PAD + T - POOL_HIST:_POOL_PAD + T, :]

        cext_s[_CONV_PAD - (CONV_K - 1):_CONV_PAD, :] = sc_ref[stream]
        cext_s[_CONV_PAD:_CONV_PAD + T, :] = cinbuf[pl.ds(r0, T), :]
        ybuf[pl.ds(r0, T), :] = _conv_taps(cext_s, T, cw_ref, cbias_ref)
        co_ref[stream] = cext_s[_CONV_PAD + T - (CONV_K - 1):_CONV_PAD + T, :]

    @pl.when(b == pl.num_programs(0) - 1)
    def _():
        h = hbuf[...]
        o_pool = _dot(dbuf[...].astype(BF16), pw_ref[...]) * ps_ref[...]
        o_conv = _dot(h, win_ref[:, _CB0:_CB0 + CONV_WIDTH]) * ybuf[...]
        side = _side_branches(h, o_pool, o_conv, win_ref, gb_ref, wp_ref, wc_ref)
        y_ref[...] = _project_out(x_ref[...], _gate(h, 0, win_ref, gb_ref), obuf[...], side,
                                  wa_ref, wo_ref)


def _mix_sample(x, cache_kt, cache_vt, state_pool, state_conv, layer, p):
    rows = x.shape[0]
    n_streams, r_cache = cache_kt.shape[1], cache_kt.shape[3]
    t_new = rows // n_streams
    in_w = p["mix_w_in"].shape[-1]
    cache_spec = pl.BlockSpec((None, _SAMPLE_STREAMS_PER_STEP, ATTN_WIDTH, r_cache),
                              lambda b: (layer, b, 0, 0))
    full = lambda shape: pl.BlockSpec(tuple(shape), lambda b: (0,) * len(shape))
    return pl.pallas_call(
        functools.partial(_mix_sample_kernel, t_new),
        grid=(n_streams // _SAMPLE_STREAMS_PER_STEP,),
        in_specs=[
            _shared_spec((rows, D_MODEL)),
            _const_spec((1, D_MODEL), layer),
            _const_spec((D_MODEL, in_w), layer),
            _const_spec((1, 3 * D_MODEL), layer),
            _const_spec((1, ATTN_WIDTH), layer),
            _const_spec((1, ATTN_WIDTH), layer),
            _shared_spec((ATTN_WIDTH, ATTN_WIDTH)),
            _const_spec((N_HEADS * t_new, r_cache), layer),
            _const_spec((N_HEADS * t_new, t_new), layer),
            _shared_spec((N_HEADS * t_new, ATTN_WIDTH)),
            cache_spec,
            cache_spec,
            _const_spec((n_streams, POOL_HIST, POOL_WIDTH), layer),
            _const_spec((n_streams, CONV_K - 1, CONV_WIDTH), layer),
            _const_spec((POOL_WIDTH, POOL_WIDTH), layer),
            _const_spec((1, POOL_WIDTH), layer),
            _const_spec((CONV_K, CONV_WIDTH), layer),
            _const_spec((1, CONV_WIDTH), layer),
            _const_spec((ATTN_WIDTH, D_MODEL), layer),
            _const_spec((POOL_WIDTH, D_MODEL), layer),
            _const_spec((CONV_WIDTH, D_MODEL), layer),
            _const_spec((D_MODEL, D_MODEL), layer),
        ],
        out_specs=[
            full((rows, D_MODEL)),
            full((rows, ATTN_WIDTH)),
            full((rows, ATTN_WIDTH)),
            full((n_streams, POOL_HIST, POOL_WIDTH)),
            full((n_streams, CONV_K - 1, CONV_WIDTH)),
        ],
        out_shape=[
            jax.ShapeDtypeStruct((rows, D_MODEL), F32),
            jax.ShapeDtypeStruct((rows, ATTN_WIDTH), F32),
            jax.ShapeDtypeStruct((rows, ATTN_WIDTH), F32),
            jax.ShapeDtypeStruct((n_streams, POOL_HIST, POOL_WIDTH), F32),
            jax.ShapeDtypeStruct((n_streams, CONV_K - 1, CONV_WIDTH), F32),
        ],
        scratch_shapes=[
            pltpu.VMEM((rows, D_MODEL), BF16),
            pltpu.VMEM((rows, ATTN_WIDTH), BF16),
            pltpu.VMEM((rows, ATTN_WIDTH), BF16),
            pltpu.VMEM((rows, ATTN_WIDTH), BF16),
            pltpu.VMEM((rows, ATTN_WIDTH), F32),
            pltpu.VMEM((rows, POOL_WIDTH), F32),
            pltpu.VMEM((rows, POOL_WIDTH), F32),
            pltpu.VMEM((rows, CONV_WIDTH), F32),
            pltpu.VMEM((rows, CONV_WIDTH), F32),
            pltpu.VMEM((_SAMPLE_STREAMS_PER_STEP, _POOL_PAD + t_new, POOL_WIDTH), F32),
            pltpu.VMEM((_SAMPLE_STREAMS_PER_STEP, _CONV_PAD + t_new, CONV_WIDTH), F32),
        ],
        compiler_params=pltpu.CompilerParams(
            dimension_semantics=("arbitrary",), vmem_limit_bytes=_VMEM_LIMIT_BYTES),
        name="mix_sample",
    )(x, p["mix_norm"], p["mix_w_in"], p["gate_b"], p["q_norm"], p["k_norm"], p["head_ones"],
      p["bias_cache"], p["bias_new"], p["head_mask"], cache_kt, cache_vt, state_pool, state_conv,
      p["pool_w"], p["pool_scale"], p["conv_w"], p["conv_b"],
      p["w_branch_attn"], p["w_branch_pool"], p["w_branch_conv"], p["w_out"])


def _rel_lookup(rel_bias, lo, hi):
    lead = rel_bias.shape[:-1]
    n_left = max(0, min(hi, -REL_CLIP) - lo)
    n_right = max(0, hi - max(lo, REL_CLIP + 1))
    a, b = max(lo, -REL_CLIP), min(hi, REL_CLIP + 1)
    parts = []
    if n_left:
        parts.append(jnp.broadcast_to(rel_bias[..., :1], lead + (n_left,)))
    if b > a:
        parts.append(rel_bias[..., a + REL_CLIP:b + REL_CLIP])
    if n_right:
        parts.append(jnp.broadcast_to(rel_bias[..., -1:], lead + (n_right,)))
    return jnp.concatenate(parts, axis=-1)


def _toeplitz(g, rows):
    n = g.shape[-1]
    lead = g.shape[:-1]
    t = jnp.broadcast_to(g[..., None, :], lead + (rows, n)).reshape(lead + (rows * n,))
    return t[..., :rows * (n - 1)].reshape(lead + (rows, n - 1))


def _band_bias(rel_bias, n_query, n_past, n_key):
    g = jnp.concatenate([_rel_lookup(rel_bias, -n_past, n_key - n_past),
                         _rel_lookup(rel_bias, -n_past - n_query, -n_past)], axis=-1)
    return _toeplitz(g, n_query)[..., :n_key].astype(F32)


def _prompt_bias(rel_bias):
    band = _band_bias(rel_bias, CHUNK, N_PREV_CHUNKS * CHUNK, BAND) * LOG2E
    masked = jnp.full(band.shape[:-1] + (KB - BAND,), NEG_INF, F32)
    return jnp.concatenate([jnp.concatenate([band, masked], axis=-1),
                            jnp.concatenate([masked, band], axis=-1)], axis=-2)


def _sample_bias(rel_bias, t_new, r_cache):
    depth = rel_bias.shape[0]
    table = _band_bias(rel_bias, t_new, r_cache, r_cache + t_new)
    table = table.reshape(depth, N_HEADS * t_new, r_cache + t_new)
    return table[..., :r_cache], table[..., r_cache:]


def kernel(x_prompt, x_sample, cache_attn_k, cache_attn_v, state_pool, state_conv, ffn1_norm, ffn1_w_in, ffn1_w_out, mix_norm, mix_w_in, gate_b, q_norm, k_norm, rel_bias, pool_w, pool_scale, conv_w, conv_b, w_branch_attn, w_branch_pool, w_branch_conv, w_out, ffn2_norm, ffn2_w_in, ffn2_w_out):
    batch, seq, _ = x_prompt.shape
    n_streams, t_new, _ = x_sample.shape
    depth = mix_w_in.shape[0]
    r_cache = cache_attn_k.shape[2]
    assert batch == 1 and seq % TM == 0 and seq >= TM
    assert r_cache == N_PREV_CHUNKS * CHUNK and PAST_LEN % CHUNK == 0 and t_new <= CHUNK
    assert n_streams % _SAMPLE_STREAMS_PER_STEP == 0

    row = lambda a: a.reshape(depth, 1, a.shape[-1])
    head = np.arange(ATTN_WIDTH) // HEAD_DIM
    n_groups = len(POOL_WINDOWS)
    pool_bd = jnp.einsum("lgcd,gh->lgchd", pool_w, jnp.eye(n_groups, dtype=pool_w.dtype))
    bias_cache, bias_new = _sample_bias(rel_bias, t_new, r_cache)
    p = dict(
        mix_norm=row(mix_norm), mix_w_in=mix_w_in.astype(BF16), gate_b=row(gate_b),
        q_norm=row(jnp.tile(q_norm, (1, N_HEADS))), k_norm=row(jnp.tile(k_norm, (1, N_HEADS))),
        head_ones=jnp.asarray(head[:, None] == head[None, :], BF16),
        bias_prompt=_prompt_bias(rel_bias), bias_cache=bias_cache, bias_new=bias_new,
        head_mask=jnp.asarray(np.repeat(np.arange(N_HEADS), t_new)[:, None] == head[None, :], F32),
        pool_w=pool_bd.reshape(depth, POOL_WIDTH, POOL_WIDTH).astype(BF16),
        pool_scale=row(pool_scale), conv_w=conv_w, conv_b=row(conv_b),
        w_branch_attn=w_branch_attn.astype(BF16), w_branch_pool=w_branch_pool.astype(BF16),
        w_branch_conv=w_branch_conv.astype(BF16), w_out=w_out.astype(BF16),
    )
    f1 = (row(ffn1_norm), ffn1_w_in.astype(BF16), ffn1_w_out.astype(BF16))
    f2 = (row(ffn2_norm), ffn2_w_in.astype(BF16), ffn2_w_out.astype(BF16))

    cache_kt = cache_attn_k.transpose(0, 1, 3, 4, 2).reshape(depth, n_streams, ATTN_WIDTH, r_cache)
    cache_vt = cache_attn_v.transpose(0, 1, 3, 4, 2).reshape(depth, n_streams, ATTN_WIDTH, r_cache)

    xp = x_prompt.reshape(seq, D_MODEL)
    xs = x_sample.reshape(n_streams * t_new, D_MODEL)
    outs = [[] for _ in range(8)]
    for l in range(depth):
        xp = _ffn(xp, *f1, l)
        xs = _ffn(xs, *f1, l)
        xp, kp, vp, pp, cp = _mix_prompt(xp, l, p)
        xs, ks, vs, ps, cs = _mix_sample(xs, cache_kt, cache_vt, state_pool, state_conv, l, p)
        xp = _ffn(xp, *f2, l)
        xs = _ffn(xs, *f2, l)
        for acc, val in zip(outs, (kp, vp, pp, cp, ks, vs, ps, cs)):
            acc.append(val)
    kp, vp, pp, cp, ks, vs, ps, cs = (jnp.stack(o) for o in outs)
    r_keep = min(N_PREV_CHUNKS * CHUNK, seq)
    return (xp.reshape(batch, seq, D_MODEL), xs.reshape(n_streams, t_new, D_MODEL),
            kp.reshape(depth, batch, r_keep, N_HEADS, HEAD_DIM),
            vp.reshape(depth, batch, r_keep, N_HEADS, HEAD_DIM),
            pp.reshape(depth, batch, POOL_HIST, POOL_WIDTH),
            cp.reshape(depth, batch, CONV_K - 1, CONV_WIDTH),
            ks.reshape(depth, n_streams, t_new, N_HEADS, HEAD_DIM),
            vs.reshape(depth, n_streams, t_new, N_HEADS, HEAD_DIM),
            ps, cs)
```
